```python
import jax
import jax.numpy as jnp
from jax import lax
import numpy as np

D_MODEL = 1024
BATCH = 4
SEQ = 8192
DEPTH = 2

GRID_W = 64
CTX_LEN = 256
N_HEADS = 8
QK_NOPE = 64
QK_ROPE = 32
V_DIM = 64
Q_LORA = 256
KV_LORA = 128
ROPE_THETA = 10000.0
ATTN_SCALE = (QK_NOPE + QK_ROPE) ** -0.5
Q_BLOCK = 128
CONV_WIDTH = 512
CONV_K = 3
N_EXPERTS = 32
TOP_K = 4
D_FF = D_MODEL
SWIGLU_LIMIT = 7.0
SWIGLU_ALPHA = 1.702
NORM_EPS = 1e-6
DEEPNORM_ALPHA = (2 * DEPTH) ** 0.25
DEEPNORM_BETA = (8 * DEPTH) ** -0.25

OFF_QA = 0
OFF_KV = OFF_QA + Q_LORA
OFF_KR = OFF_KV + KV_LORA
OFF_CX = OFF_KR + QK_ROPE
OFF_CB = OFF_CX + CONV_WIDTH
OFF_CC = OFF_CB + CONV_WIDTH
OFF_GA = OFF_CC + CONV_WIDTH
OFF_GB = OFF_GA + D_MODEL
IN_WIDTH = OFF_GB + D_MODEL

kernel_name = "hybrid_mla_shortconv_moe_dit"


def layer_norm(x, g, b):
    xf = x.astype(jnp.float32)
    mu = jnp.mean(xf, axis=-1, keepdims=True)
    var = jnp.mean(jnp.square(xf - mu), axis=-1, keepdims=True)
    return ((xf - mu) * lax.rsqrt(var + NORM_EPS) * g + b).astype(x.dtype)


def rms_norm(x, g):
    xf = x.astype(jnp.float32)
    return (xf * lax.rsqrt(jnp.mean(jnp.square(xf), axis=-1, keepdims=True) + NORM_EPS) * g).astype(x.dtype)


def axial_rope_tables(n, dtype):
    n_rows = n // GRID_W
    row = jnp.broadcast_to(jnp.arange(n_rows, dtype=jnp.float32)[:, None], (n_rows, GRID_W)).reshape(-1)
    col = jnp.broadcast_to(jnp.arange(GRID_W, dtype=jnp.float32)[None, :], (n_rows, GRID_W)).reshape(-1)
    half = QK_ROPE // 2
    freqs = ROPE_THETA ** (-jnp.arange(0, half, 2, dtype=jnp.float32) / half)
    ang_r = row[:, None] * freqs
    ang_c = col[:, None] * freqs
    ang = jnp.concatenate([ang_r, ang_r, ang_c, ang_c], axis=-1)
    return jnp.cos(ang).astype(dtype), jnp.sin(ang).astype(dtype)


def rotate_axial(x, cos, sin):
    x1, x2, x3, x4 = jnp.split(x, 4, axis=-1)
    rot = jnp.concatenate([-x2, x1, -x4, x3], axis=-1)
    return x * cos + rot * sin


def mla_queries(q_a, q_norm, w_uq):
    q = rms_norm(q_a, q_norm) @ w_uq
    q = q.reshape(*q_a.shape[:-1], N_HEADS, QK_NOPE + QK_ROPE)
    return q[..., :QK_NOPE], q[..., QK_NOPE:]


def mla_kv(kv_a, kv_norm, w_uk, w_uv):
    ckv = rms_norm(kv_a, kv_norm)
    shp = kv_a.shape[:-1]
    return (ckv @ w_uk).reshape(*shp, N_HEADS, QK_NOPE), (ckv @ w_uv).reshape(*shp, N_HEADS, V_DIM)


def mla_attend(q_nope, q_rope, k_nope, k_rope, v):
    s = jnp.einsum('bqhd,bkhd->bhqk', q_nope, k_nope, preferred_element_type=jnp.float32)
    s = s + jnp.einsum('bqhr,bkr->bhqk', q_rope, k_rope, preferred_element_type=jnp.float32)
    p = jax.nn.softmax(s * ATTN_SCALE, axis=-1)
    return jnp.einsum('bhqk,bkhd->bqhd', p.astype(v.dtype), v)


def latent_attention(q_nope, q_rope, k_nope, k_rope, v):
    b, n = q_nope.shape[:2]
    nb = n // Q_BLOCK

    def blocks(t):
        return jnp.moveaxis(t.reshape(b, nb, Q_BLOCK, *t.shape[2:]), 1, 0)

    out = lax.map(lambda qs: mla_attend(qs[0], qs[1], k_nope, k_rope, v), (blocks(q_nope), blocks(q_rope)))
    return jnp.moveaxis(out, 0, 1).reshape(b, n, N_HEADS * V_DIM)


def short_conv(u, b_gate, c_gate, w, bias):
    n = u.shape[1]
    pad = CONV_K // 2
    z = jnp.pad(c_gate * u, ((0, 0), (pad, pad), (0, 0)))
    y = sum(z[:, k:k + n] * w[k] for k in range(CONV_K)) + bias
    return b_gate * y


def merge_branches(p, attn, conv_w, conv_b, w_oa, w_ob, w_o):
    conv = short_conv(p[..., OFF_CX:OFF_CB], p[..., OFF_CB:OFF_CC], p[..., OFF_CC:OFF_GA], conv_w, conv_b)
    y_a = attn @ w_oa
    y_b = conv @ w_ob
    merged = jax.nn.sigmoid(p[..., OFF_GA:OFF_GB]) * y_a + jax.nn.sigmoid(p[..., OFF_GB:IN_WIDTH]) * y_b
    return merged @ w_o


def token_mixer(h, hc, cos, sin, with_ctx_out, w_in, b_in, q_norm, kv_norm, w_uq, w_uk, w_uv,
                w_oa, conv_w, conv_b, w_ob, w_o):
    p = h @ w_in + b_in
    if with_ctx_out:
        pc = hc @ w_in + b_in
        kv_c = pc[..., OFF_KV:OFF_CX]
    else:
        kv_c = hc @ w_in[:, OFF_KV:OFF_CX] + b_in[OFF_KV:OFF_CX]
    kc_nope, vc = mla_kv(kv_c[..., :KV_LORA], kv_norm, w_uk, w_uv)
    kc_rope = kv_c[..., KV_LORA:]
    k_nope, v = mla_kv(p[..., OFF_KV:OFF_KR], kv_norm, w_uk, w_uv)
    k_rope = rotate_axial(p[..., OFF_KR:OFF_CX], cos, sin)
    q_nope, q_rope = mla_queries(p[..., OFF_QA:OFF_KV], q_norm, w_uq)
    q_rope = rotate_axial(q_rope, cos[:, None], sin[:, None])
    attn = latent_attention(q_nope, q_rope,
                            jnp.concatenate([kc_nope, k_nope], axis=1),
                            jnp.concatenate([kc_rope, k_rope], axis=1),
                            jnp.concatenate([vc, v], axis=1))
    y = merge_branches(p, attn, conv_w, conv_b, w_oa, w_ob, w_o)
    if not with_ctx_out:
        return y, None
    qc_nope, qc_rope = mla_queries(pc[..., OFF_QA:OFF_KV], q_norm, w_uq)
    attn_c = mla_attend(qc_nope, qc_rope, kc_nope, kc_rope, vc)
    attn_c = attn_c.reshape(*attn_c.shape[:2], N_HEADS * V_DIM)
    y_c = merge_branches(pc, attn_c, conv_w, conv_b, w_oa, w_ob, w_o)
    return y, y_c


def moe_ffn(t, w_router, b_router, w_gu, b_gu, w_down, b_down):
    logits = jnp.dot(t, w_router, preferred_element_type=jnp.float32) + b_router.astype(jnp.float32)
    top_logit, top_idx = lax.top_k(logits, TOP_K)
    top_p = jax.nn.softmax(top_logit, axis=-1)
    gates = jnp.einsum('tk,tke->te', top_p, jax.nn.one_hot(top_idx, N_EXPERTS, dtype=jnp.float32)).astype(t.dtype)

    def expert(acc, xs):
        wgu, bgu, wd, bd, g = xs
        gu = t @ wgu + bgu
        gate = jnp.minimum(gu[:, :D_FF], SWIGLU_LIMIT)
        up = jnp.clip(gu[:, D_FF:], -SWIGLU_LIMIT, SWIGLU_LIMIT)
        y = ((up + 1.0) * gate * jax.nn.sigmoid(SWIGLU_ALPHA * gate)) @ wd + bd
        return acc + g[:, None] * y, None

    out, _ = lax.scan(expert, jnp.zeros_like(t), (w_gu, b_gu, w_down, b_down, gates.T))
    return out


def setup_inputs(seed: int = 0) -> dict:
    key = jax.random.key(seed)
    ks = jax.random.split(key, 32)
    f32 = jnp.float32
    beta = DEEPNORM_BETA

    def nrm(k, shape, scale):
        return jax.random.normal(k, shape, f32) * scale

    L = DEPTH
    return {
        'x': nrm(ks[0], (BATCH, SEQ, D_MODEL), 1.0),
        'c': nrm(ks[1], (BATCH, D_MODEL), 1.0),
        'ctx': nrm(ks[2], (BATCH, CTX_LEN, D_MODEL), 1.0),
        'c_ctx': nrm(ks[3], (D_MODEL,), 1.0),
        'w_ada': nrm(ks[4], (L, D_MODEL, 6 * D_MODEL), 0.5 * D_MODEL ** -0.5),
        'b_ada': nrm(ks[5], (L, 6 * D_MODEL), 0.02),
        'w_in': nrm(ks[6], (L, D_MODEL, IN_WIDTH), D_MODEL ** -0.5),
        'b_in': nrm(ks[7], (L, IN_WIDTH), 0.02),
        'q_norm': 1.0 + nrm(ks[8], (L, Q_LORA), 0.02),
        'kv_norm': 1.0 + nrm(ks[9], (L, KV_LORA), 0.02),
        'w_uq': nrm(ks[10], (L, Q_LORA, N_HEADS * (QK_NOPE + QK_ROPE)), Q_LORA ** -0.5),
        'w_uk': nrm(ks[11], (L, KV_LORA, N_HEADS * QK_NOPE), KV_LORA ** -0.5),
        'w_uv': nrm(ks[12], (L, KV_LORA, N_HEADS * V_DIM), beta * KV_LORA ** -0.5),
        'w_oa': nrm(ks[13], (L, N_HEADS * V_DIM, D_MODEL), beta * (N_HEADS * V_DIM) ** -0.5),
        'conv_w': nrm(ks[14], (L, CONV_K, CONV_WIDTH), CONV_K ** -0.5),
        'conv_b': nrm(ks[15], (L, CONV_WIDTH), 0.02),
        'w_ob': nrm(ks[16], (L, CONV_WIDTH, D_MODEL), beta * CONV_WIDTH ** -0.5),
        'w_o': nrm(ks[17], (L, D_MODEL, D_MODEL), beta * D_MODEL ** -0.5),
        'ln1_g': 1.0 + nrm(ks[18], (L, D_MODEL), 0.02),
        'ln1_b': nrm(ks[19], (L, D_MODEL), 0.02),
        'ln2_g': 1.0 + nrm(ks[20], (L, D_MODEL), 0.02),
        'ln2_b': nrm(ks[21], (L, D_MODEL), 0.02),
        'w_router': nrm(ks[22], (L, D_MODEL, N_EXPERTS), D_MODEL ** -0.5),
        'b_router': nrm(ks[23], (L, N_EXPERTS), 0.01),
        'w_gu': nrm(ks[24], (L, N_EXPERTS, D_MODEL, 2 * D_FF), beta * D_MODEL ** -0.5),
        'b_gu': nrm(ks[25], (L, N_EXPERTS, 2 * D_FF), 0.02),
        'w_down': nrm(ks[26], (L, N_EXPERTS, D_FF, D_MODEL), beta * D_FF ** -0.5),
        'b_down': nrm(ks[27], (L, N_EXPERTS, D_MODEL), 0.02),
    }


def reference(x, c, ctx, c_ctx, w_ada, b_ada, w_in, b_in, q_norm, kv_norm, w_uq, w_uk, w_uv, w_oa,
              conv_w, conv_b, w_ob, w_o, ln1_g, ln1_b, ln2_g, ln2_b, w_router, b_router,
              w_gu, b_gu, w_down, b_down):
    b, n, d = x.shape
    cos, sin = axial_rope_tables(n, x.dtype)
    silu_c = jax.nn.silu(c)
    silu_cc = jax.nn.silu(c_ctx)
    alpha = DEEPNORM_ALPHA
    for l in range(DEPTH):
        update_ctx = l < DEPTH - 1
        mod = silu_c @ w_ada[l] + b_ada[l]
        mod_c = silu_cc @ w_ada[l] + b_ada[l]
        sh1, sc1, g1, sh2, sc2, g2 = jnp.split(mod[:, None, :], 6, axis=-1)
        sh1c, sc1c, g1c, sh2c, sc2c, g2c = jnp.split(mod_c, 6)
        h = x * (1.0 + sc1) + sh1
        hc = ctx * (1.0 + sc1c) + sh1c
        y, y_c = token_mixer(h, hc, cos, sin, update_ctx, w_in[l], b_in[l], q_norm[l], kv_norm[l],
                             w_uq[l], w_uk[l], w_uv[l], w_oa[l], conv_w[l], conv_b[l], w_ob[l], w_o[l])
        x = layer_norm(alpha * x + g1 * y, ln1_g[l], ln1_b[l])
        if update_ctx:
            ctx = layer_norm(alpha * ctx + g1c * y_c, ln1_g[l], ln1_b[l])
        h = (x * (1.0 + sc2) + sh2).reshape(b * n, d)
        if update_ctx:
            hc = (ctx * (1.0 + sc2c) + sh2c).reshape(-1, d)
            f = moe_ffn(jnp.concatenate([h, hc], axis=0), w_router[l], b_router[l],
                        w_gu[l], b_gu[l], w_down[l], b_down[l])
            f_x = f[:b * n]
            ctx = layer_norm(alpha * ctx + g2c * f[b * n:].reshape(ctx.shape), ln2_g[l], ln2_b[l])
        else:
            f_x = moe_ffn(h, w_router[l], b_router[l], w_gu[l], b_gu[l], w_down[l], b_down[l])
        x = layer_norm(alpha * x + g2 * f_x.reshape(b, n, d), ln2_g[l], ln2_b[l])
    return x
```

```python
import functools

import jax
import jax.numpy as jnp
from jax import lax
from jax.experimental import pallas as pl
from jax.experimental.pallas import tpu as pltpu

F32 = jnp.float32
BF16 = jnp.bfloat16
I32 = jnp.int32

D_MODEL = 1024
DEPTH = 2
GRID_W = 64
N_HEADS = 8
QK_NOPE = 64
QK_ROPE = 32
V_DIM = 64
Q_LORA = 256
KV_LORA = 128
ROPE_THETA = 10000.0
ATTN_SCALE = (QK_NOPE + QK_ROPE) ** -0.5
CONV_WIDTH = 512
CONV_K = 3
N_EXPERTS = 32
TOP_K = 4
D_FF = D_MODEL
SWIGLU_LIMIT = 7.0
SWIGLU_ALPHA = 1.702
NORM_EPS = 1e-6
DEEPNORM_ALPHA = (2 * DEPTH) ** 0.25

OFF_QA = 0
OFF_KV = OFF_QA + Q_LORA
OFF_KR = OFF_KV + KV_LORA
OFF_CX = OFF_KR + QK_ROPE
OFF_CB = OFF_CX + CONV_WIDTH
OFF_CC = OFF_CB + CONV_WIDTH
OFF_GA = OFF_CC + CONV_WIDTH
OFF_GB = OFF_GA + D_MODEL
IN_WIDTH = OFF_GB + D_MODEL

LANES = 128
QK_WIDTH = 2 * LANES
LOG2E = 1.4426950408889634
NEG_BIG = -1e30
VMEM_LIMIT = 56 * 1024 * 1024

PA_W = Q_LORA + KV_LORA + 2 * LANES
PC_OFF = PA_W
PC_W = 3 * CONV_WIDTH
PG_OFF = PC_OFF + PC_W
PG_W = 2 * D_MODEL
PACK_W = PG_OFF + PG_W

ROW_TILE = 512
Q_TILE = 1024
KV_CHUNK = 768
EXPERT_TILE = 512
COMBINE_TILE = 256


def _dot(a, b):
    return jnp.dot(a, b, preferred_element_type=F32)


def _split_bf16(a):
    hi = a.astype(BF16)
    lo = (a - hi.astype(F32)).astype(BF16)
    return hi, lo


def _dot3(a, b):
    ah, al = _split_bf16(a)
    bh, bl = _split_bf16(b)
    return _dot(ah, bh) + (_dot(ah, bl) + _dot(al, bh))


def _rms_norm(x, g):
    return x * lax.rsqrt(jnp.mean(jnp.square(x), axis=-1, keepdims=True) + NORM_EPS) * g


def _layer_norm(x, g, b):
    mu = jnp.mean(x, axis=-1, keepdims=True)
    xc = x - mu
    var = jnp.mean(jnp.square(xc), axis=-1, keepdims=True)
    return xc * lax.rsqrt(var + NORM_EPS) * g + b


def _const_spec(shape):
    nd = len(shape)
    return pl.BlockSpec(shape, lambda *_: (0,) * nd, pipeline_mode=pl.Buffered(1))


def _mod_kernel(c_ref, w_ref, b_ref, o_ref):
    c = c_ref[...]
    s = c * jax.nn.sigmoid(c)
    o_ref[0] = _dot3(s, w_ref[0]) + b_ref[0]


def _modulation(cvec, w_ada, b_ada):
    depth, d, n = w_ada.shape
    tn = 1536
    return pl.pallas_call(
        _mod_kernel,
        out_shape=jax.ShapeDtypeStruct((depth, 8, n), F32),
        grid=(depth, n // tn),
        in_specs=[
            pl.BlockSpec((8, d), lambda l, j: (0, 0)),
            pl.BlockSpec((1, d, tn), lambda l, j: (l, 0, j)),
            pl.BlockSpec((1, 1, tn), lambda l, j: (l, 0, j)),
        ],
        out_specs=pl.BlockSpec((1, 8, tn), lambda l, j: (l, 0, j)),
        compiler_params=pltpu.CompilerParams(
            dimension_semantics=("arbitrary", "arbitrary"), vmem_limit_bytes=VMEM_LIMIT),
        name="adaln_mod",
    )(cvec, w_ada, b_ada.reshape(depth, 1, n))


def _inproj_kernel(x_ref, scp_ref, sh_ref, cos_ref, sin_ref, w_ref, b_ref, qn_ref, kvn_ref,
                   wq_ref, bduk_ref,
                   q_ref, kt_ref, v_ref, zc_ref, bg_ref, gs_ref):
    h = (x_ref[0] * scp_ref[0] + sh_ref[0]).astype(BF16)
    cos = cos_ref[...]
    sin = sin_ref[...]

    pa = _dot(h, w_ref[:, 0:PA_W]) + b_ref[:, 0:PA_W]
    qn = _rms_norm(pa[:, 0:Q_LORA], qn_ref[...]).astype(BF16)
    ckv = _rms_norm(pa[:, Q_LORA:Q_LORA + KV_LORA], kvn_ref[...])
    kr = pa[:, Q_LORA + KV_LORA:Q_LORA + KV_LORA + LANES]
    krr = pa[:, Q_LORA + KV_LORA + LANES:PA_W]
    k_rope = kr * cos + krr * sin
    k_row = jnp.concatenate([ckv, k_rope], axis=1)
    kt_ref[0] = k_row.T.astype(BF16)
    v_ref[0] = ckv.astype(BF16)

    nope_w = N_HEADS * QK_NOPE
    blk_w = N_HEADS * LANES
    q_all = _dot(qn, wq_ref[...])
    q_lat = _dot(q_all[:, 0:nope_w].astype(BF16), bduk_ref[...])
    qscale = ATTN_SCALE * LOG2E
    for hd in range(N_HEADS):
        lo = hd * LANES
        qr = q_all[:, nope_w + lo:nope_w + lo + LANES]
        qrr = q_all[:, nope_w + blk_w + lo:nope_w + blk_w + lo + LANES]
        q_rope = qr * cos + qrr * sin
        q_row = jnp.concatenate([q_lat[:, lo:lo + LANES], q_rope], axis=1) * qscale
        q_ref[0, hd] = q_row.astype(BF16)

    pc = _dot(h, w_ref[:, PC_OFF:PC_OFF + PC_W]) + b_ref[:, PC_OFF:PC_OFF + PC_W]
    zc_ref[0] = pc[:, 2 * CONV_WIDTH:3 * CONV_WIDTH] * pc[:, 0:CONV_WIDTH]
    bg_ref[0] = pc[:, CONV_WIDTH:2 * CONV_WIDTH]

    pg = _dot(h, w_ref[:, PG_OFF:PG_OFF + PG_W]) + b_ref[:, PG_OFF:PG_OFF + PG_W]
    gs_ref[0] = jax.nn.sigmoid(pg).astype(BF16)


def _in_projection(x, scp, sh, cos, sin, lw, tm):
    b, s, d = x.shape
    row = lambda bi, i: (bi, i, 0)
    per_b = lambda bi, i: (bi, 0, 0)
    return pl.pallas_call(
        _inproj_kernel,
        out_shape=(
            jax.ShapeDtypeStruct((b, N_HEADS, s, QK_WIDTH), BF16),
            jax.ShapeDtypeStruct((b, QK_WIDTH, s), BF16),
            jax.ShapeDtypeStruct((b, s, KV_LORA), BF16),
            jax.ShapeDtypeStruct((b, s, CONV_WIDTH), F32),
            jax.ShapeDtypeStruct((b, s, CONV_WIDTH), F32),
            jax.ShapeDtypeStruct((b, s, 2 * D_MODEL), BF16),
        ),
        grid=(b, s // tm),
        in_specs=[
            pl.BlockSpec((1, tm, d), row),
            pl.BlockSpec((1, 1, d), per_b),
            pl.BlockSpec((1, 1, d), per_b),
            pl.BlockSpec((tm, LANES), lambda bi, i: (i, 0)),
            pl.BlockSpec((tm, LANES), lambda bi, i: (i, 0)),
            _const_spec((d, PACK_W)),
            _const_spec((1, PACK_W)),
            _const_spec((1, Q_LORA)),
            _const_spec((1, KV_LORA)),
            _const_spec(lw["wq"].shape),
            _const_spec(lw["bduk"].shape),
        ],
        out_specs=(
            pl.BlockSpec((1, N_HEADS, tm, QK_WIDTH), lambda bi, i: (bi, 0, i, 0)),
            pl.BlockSpec((1, QK_WIDTH, tm), lambda bi, i: (bi, 0, i)),
            pl.BlockSpec((1, tm, KV_LORA), row),
            pl.BlockSpec((1, tm, CONV_WIDTH), row),
            pl.BlockSpec((1, tm, CONV_WIDTH), row),
            pl.BlockSpec((1, tm, 2 * D_MODEL), row),
        ),
        compiler_params=pltpu.CompilerParams(
            dimension_semantics=("arbitrary", "arbitrary"), vmem_limit_bytes=VMEM_LIMIT),
        name="in_projection",
    )(x, scp, sh, cos, sin, lw["w_pack"], lw["b_pack"], lw["q_norm"], lw["kv_norm"],
      lw["wq"], lw["bduk"])


def _attn_kernel(q_ref, kt_ref, v_ref, o_ref, *, tk, nchunks):
    q = q_ref[0, 0]
    tq = q.shape[0]

    def body(c, carry):
        m, l, acc = carry
        off = pl.multiple_of(c * tk, tk)
        s = _dot(q, kt_ref[0, :, pl.ds(off, tk)])
        m_new = jnp.maximum(m, jnp.max(s, axis=-1, keepdims=True))
        p = jnp.exp2(s - m_new)
        a = jnp.exp2(m - m_new)
        l = a * l + jnp.sum(p, axis=-1, keepdims=True)
        acc = a * acc + _dot(p.astype(BF16), v_ref[0, pl.ds(off, tk), :])
        return m_new, l, acc

    init = (jnp.full((tq, 1), NEG_BIG, F32), jnp.zeros((tq, 1), F32), jnp.zeros((tq, KV_LORA), F32))
    _, l, acc = lax.fori_loop(0, nchunks, body, init)
    o_ref[0] = (acc / l).astype(BF16)


def _latent_attention(q, kt, v, tq, tk):
    b, nh, s, _ = q.shape
    skv = kt.shape[2]
    return pl.pallas_call(
        functools.partial(_attn_kernel, tk=tk, nchunks=skv // tk),
        out_shape=jax.ShapeDtypeStruct((b, s, nh * KV_LORA), BF16),
        grid=(b, nh, s // tq),
        in_specs=[
            pl.BlockSpec((1, 1, tq, QK_WIDTH), lambda bi, hi, i: (bi, hi, i, 0)),
            pl.BlockSpec((1, QK_WIDTH, skv), lambda bi, hi, i: (bi, 0, 0)),
            pl.BlockSpec((1, skv, KV_LORA), lambda bi, hi, i: (bi, 0, 0)),
        ],
        out_specs=pl.BlockSpec((1, tq, KV_LORA), lambda bi, hi, i: (bi, i, hi)),
        compiler_params=pltpu.CompilerParams(
            dimension_semantics=("arbitrary", "arbitrary", "arbitrary"),
            vmem_limit_bytes=VMEM_LIMIT),
        name="latent_attention",
    )(q, kt, v)


def _merge_kernel(o_ref, zc_ref, zp_ref, zn_ref, bg_ref, gs_ref, x_ref, g1_ref, scp2_ref, sh2_ref,
                  bduv_ref, woa_ref, wob_ref, wo_ref, cw_ref, cb_ref, lng_ref, lnb_ref,
                  wr_ref, br_ref,
                  x1_ref, h2_ref, ridx_ref, rgate_ref, *, ntiles):
    i = pl.program_id(1)
    tm = x_ref.shape[1]

    attn = _dot(o_ref[0], bduv_ref[...]).astype(BF16)
    y_a = _dot(attn, woa_ref[...])

    zc = zc_ref[0]
    rows = lax.broadcasted_iota(I32, zc.shape, 0)
    prev_row = jnp.where(i > 0, zp_ref[0][7:8, :], 0.0)
    next_row = jnp.where(i < ntiles - 1, zn_ref[0][0:1, :], 0.0)
    z_prev = jnp.where(rows == 0, prev_row, pltpu.roll(zc, 1, axis=0))
    z_next = jnp.where(rows == tm - 1, next_row, pltpu.roll(zc, tm - 1, axis=0))
    cw = cw_ref[...]
    conv = bg_ref[0] * (z_prev * cw[0:1] + zc * cw[1:2] + z_next * cw[2:3] + cb_ref[...])
    y_b = _dot(conv.astype(BF16), wob_ref[...])

    gs = gs_ref[0].astype(F32)
    merged = gs[:, 0:D_MODEL] * y_a + gs[:, D_MODEL:2 * D_MODEL] * y_b
    y = _dot(merged.astype(BF16), wo_ref[...])

    x1 = _layer_norm(DEEPNORM_ALPHA * x_ref[0] + g1_ref[0] * y, lng_ref[...], lnb_ref[...])
    x1_ref[0] = x1
    h2 = x1 * scp2_ref[0] + sh2_ref[0]
    h2_ref[0] = h2

    logits = _dot3(h2, wr_ref[...]) + br_ref[...]
    lanes = lax.broadcasted_iota(I32, logits.shape, 1).astype(F32)
    idx_out = jnp.zeros(logits.shape, F32)
    val_out = jnp.zeros(logits.shape, F32)
    top = None
    denom = None
    for k in range(TOP_K):
        mx = jnp.max(logits, axis=-1, keepdims=True)
        sel = jnp.min(jnp.where(logits == mx, lanes, float(LANES)), axis=-1, keepdims=True)
        if k == 0:
            top = mx
            e = jnp.ones_like(mx)
            denom = e
        else:
            e = jnp.exp(mx - top)
            denom = denom + e
        idx_out = jnp.where(lanes == k, sel, idx_out)
        val_out = jnp.where(lanes == k, e, val_out)
        logits = jnp.where(lanes == sel, NEG_BIG * 2.0, logits)
    ridx_ref[0] = idx_out.astype(I32)
    rgate_ref[0] = val_out / denom


def _merge_ln_route(o, zc, bg, gs, x, g1, scp2, sh2, lw, tm):
    b, s, d = x.shape
    ntiles = s // tm
    h8 = tm // 8
    row = lambda bi, i: (bi, i, 0)
    per_b = lambda bi, i: (bi, 0, 0)
    consts = [lw["bduv"], lw["w_oa"], lw["w_ob"], lw["w_o"], lw["conv_w"], lw["conv_b"],
              lw["ln1_g"], lw["ln1_b"], lw["w_router"], lw["b_router"]]
    return pl.pallas_call(
        functools.partial(_merge_kernel, ntiles=ntiles),
        out_shape=(
            jax.ShapeDtypeStruct((b, s, d), F32),
            jax.ShapeDtypeStruct((b, s, d), F32),
            jax.ShapeDtypeStruct((b, s, LANES), I32),
            jax.ShapeDtypeStruct((b, s, LANES), F32),
        ),
        grid=(b, ntiles),
        in_specs=[
            pl.BlockSpec((1, tm, N_HEADS * KV_LORA), row),
            pl.BlockSpec((1, tm, CONV_WIDTH), row),
            pl.BlockSpec((1, 8, CONV_WIDTH), lambda bi, i: (bi, jnp.maximum(i * h8 - 1, 0), 0)),
            pl.BlockSpec((1, 8, CONV_WIDTH),
                         lambda bi, i: (bi, jnp.minimum((i + 1) * h8, s // 8 - 1), 0)),
            pl.BlockSpec((1, tm, CONV_WIDTH), row),
            pl.BlockSpec((1, tm, 2 * D_MODEL), row),
            pl.BlockSpec((1, tm, d), row),
            pl.BlockSpec((1, 1, d), per_b),
            pl.BlockSpec((1, 1, d), per_b),
            pl.BlockSpec((1, 1, d), per_b),
        ] + [_const_spec(w.shape) for w in consts],
        out_specs=(
            pl.BlockSpec((1, tm, d), row),
            pl.BlockSpec((1, tm, d), row),
            pl.BlockSpec((1, tm, LANES), row),
            pl.BlockSpec((1, tm, LANES), row),
        ),
        compiler_params=pltpu.CompilerParams(
            dimension_semantics=("arbitrary", "arbitrary"), vmem_limit_bytes=VMEM_LIMIT),
        name="merge_ln_route",
    )(o, zc, zc, zc, bg, gs, x, g1, scp2, sh2, *consts)


def _dispatch_kernel(idx_ref, h_hbm, o_ref, idx_smem, sem_idx, sem_rows):
    rows = o_ref.shape[0]
    cp = pltpu.make_async_copy(idx_ref.at[0], idx_smem, sem_idx.at[0])
    cp.start()
    cp.wait()

    def issue(r, carry):
        t = idx_smem[0, r]
        pltpu.make_async_copy(h_hbm.at[pl.ds(t, 1)], o_ref.at[pl.ds(r, 1)], sem_rows.at[0]).start()
        return carry

    lax.fori_loop(0, rows, issue, 0)
    pltpu.make_async_copy(h_hbm.at[pl.ds(0, rows)], o_ref, sem_rows.at[0]).wait()


def _dispatch(src, h, rows):
    nsteps = src.shape[0] // rows
    d = h.shape[1]
    return pl.pallas_call(
        _dispatch_kernel,
        out_shape=jax.ShapeDtypeStruct((nsteps * rows, d), h.dtype),
        grid=(nsteps,),
        in_specs=[
            pl.BlockSpec((1, 1, rows), lambda i: (i, 0, 0)),
            pl.BlockSpec(memory_space=pl.ANY),
        ],
        out_specs=pl.BlockSpec((rows, d), lambda i: (i, 0)),
        scratch_shapes=[
            pltpu.SMEM((1, rows), I32),
            pltpu.SemaphoreType.DMA((1,)),
            pltpu.SemaphoreType.DMA((1,)),
        ],
        compiler_params=pltpu.CompilerParams(
            dimension_semantics=("arbitrary",), vmem_limit_bytes=VMEM_LIMIT),
        name="moe_dispatch",
    )(src.reshape(nsteps, 1, rows), h)


def _expert_kernel(te_ref, tv_ref, x_ref, g_ref, wgu_ref, bgu_ref, wd_ref, bd_ref, y_ref):
    j = pl.program_id(0)

    @pl.when(tv_ref[j] == 1)
    def _():
        xb = x_ref[...].astype(BF16)
        gu = _dot(xb, wgu_ref[0]) + bgu_ref[0]
        gate = jnp.minimum(gu[:, 0:D_FF], SWIGLU_LIMIT)
        up = jnp.clip(gu[:, D_FF:2 * D_FF], -SWIGLU_LIMIT, SWIGLU_LIMIT)
        act = (up + 1.0) * gate * jax.nn.sigmoid(SWIGLU_ALPHA * gate)
        y = _dot(act.astype(BF16), wd_ref[0]) + bd_ref[0]
        y_ref[...] = g_ref[...] * y

    @pl.when(tv_ref[j] != 1)
    def _():
        y_ref[...] = jnp.zeros_like(y_ref)


def _expert_ffn(te, tv, xs, gsort, wgu, bgu, wd, bd, tme):
    a, d = xs.shape
    ne, _, f2 = wgu.shape
    f = wd.shape[1]
    grid_spec = pltpu.PrefetchScalarGridSpec(
        num_scalar_prefetch=2,
        grid=(a // tme,),
        in_specs=[
            pl.BlockSpec((tme, d), lambda j, te, tv: (j, 0)),
            pl.BlockSpec((tme, 1), lambda j, te, tv: (j, 0)),
            pl.BlockSpec((1, d, f2), lambda j, te, tv: (te[j], 0, 0)),
            pl.BlockSpec((1, 1, f2), lambda j, te, tv: (te[j], 0, 0)),
            pl.BlockSpec((1, f, d), lambda j, te, tv: (te[j], 0, 0)),
            pl.BlockSpec((1, 1, d), lambda j, te, tv: (te[j], 0, 0)),
        ],
        out_specs=pl.BlockSpec((tme, d), lambda j, te, tv: (j, 0)),
    )
    return pl.pallas_call(
        _expert_kernel,
        out_shape=jax.ShapeDtypeStruct((a, d), F32),
        grid_spec=grid_spec,
        compiler_params=pltpu.CompilerParams(
            dimension_semantics=("arbitrary",), vmem_limit_bytes=VMEM_LIMIT),
        name="expert_ffn",
    )(te, tv, xs, gsort, wgu, bgu.reshape(ne, 1, f2), wd, bd.reshape(ne, 1, d))


def _combine_kernel(pos_ref, y_hbm, x1_ref, g2_ref, lng_ref, lnb_ref, o_ref,
                    buf, pos_smem, sem_idx, sem_rows):
    tc = x1_ref.shape[0]
    cp = pltpu.make_async_copy(pos_ref.at[0], pos_smem, sem_idx.at[0])
    cp.start()
    cp.wait()

    for k in range(TOP_K):
        def issue(t, carry, k=k):
            p = pos_smem[0, k * tc + t]
            pltpu.make_async_copy(y_hbm.at[pl.ds(p, 1)], buf.at[k, pl.ds(t, 1)],
                                  sem_rows.at[0]).start()
            return carry
        lax.fori_loop(0, tc, issue, 0)
    for k in range(TOP_K):
        pltpu.make_async_copy(y_hbm.at[pl.ds(0, tc)], buf.at[k], sem_rows.at[0]).wait()

    f = (buf[0] + buf[1]) + (buf[2] + buf[3])
    r = DEEPNORM_ALPHA * x1_ref[...] + g2_ref[0] * f
    o_ref[...] = _layer_norm(r, lng_ref[...], lnb_ref[...])


def _combine_ln(pos_tiles, ys, x1, g2, rows_per_mod, ln_g, ln_b, tc):
    t, d = x1.shape
    nsteps = t // tc
    return pl.pallas_call(
        _combine_kernel,
        out_shape=jax.ShapeDtypeStruct((t, d), F32),
        grid=(nsteps,),
        in_specs=[
            pl.BlockSpec((1, 1, TOP_K * tc), lambda i: (i, 0, 0)),
            pl.BlockSpec(memory_space=pl.ANY),
            pl.BlockSpec((tc, d), lambda i: (i, 0)),
            pl.BlockSpec((1, 1, d), lambda i: ((i * tc) // rows_per_mod, 0, 0)),
            _const_spec((1, d)),
            _const_spec((1, d)),
        ],
        out_specs=pl.BlockSpec((tc, d), lambda i: (i, 0)),
        scratch_shapes=[
            pltpu.VMEM((TOP_K, tc, d), F32),
            pltpu.SMEM((1, TOP_K * tc), I32),
            pltpu.SemaphoreType.DMA((1,)),
            pltpu.SemaphoreType.DMA((1,)),
        ],
        compiler_params=pltpu.CompilerParams(
            dimension_semantics=("arbitrary",), vmem_limit_bytes=VMEM_LIMIT),
        name="moe_combine_ln",
    )(pos_tiles, ys, x1, g2, ln_g, ln_b)


def _routing_tables(ridx, rgate, tme):
    t = ridx.shape[0]
    experts = jnp.arange(N_EXPERTS, dtype=I32)
    onehot = jnp.sum((ridx[:, :, None] == experts[None, None, :]).astype(I32), axis=1)
    csum = jnp.cumsum(onehot, axis=0)
    rank = csum - onehot
    counts = csum[-1]
    padded = ((counts + tme - 1) // tme) * tme
    ends = jnp.cumsum(padded)
    starts = ends - padded
    pos = starts[ridx] + jnp.take_along_axis(rank, ridx, axis=1)
    ntiles = -(-(TOP_K * t) // tme) + N_EXPERTS
    nslots = ntiles * tme
    flat = pos.reshape(-1)
    tok = jnp.repeat(jnp.arange(t, dtype=I32), TOP_K)
    src = jnp.zeros((nslots,), I32).at[flat].set(tok)
    gsort = jnp.zeros((nslots,), F32).at[flat].set(rgate.reshape(-1))
    tile_start = jnp.arange(ntiles, dtype=I32) * tme
    te = jnp.minimum(jnp.searchsorted(ends, tile_start, side="right").astype(I32), N_EXPERTS - 1)
    tv = (tile_start < ends[-1]).astype(I32)
    return pos, src, gsort.reshape(nslots, 1), te, tv


def _pos_tiles(pos, tc):
    t = pos.shape[0]
    return pos.reshape(t // tc, tc, TOP_K).transpose(0, 2, 1).reshape(t // tc, 1, TOP_K * tc)


def _rot_cols(w):
    q = QK_ROPE // 4
    w1, w2, w3, w4 = w[..., 0:q], w[..., q:2 * q], w[..., 2 * q:3 * q], w[..., 3 * q:4 * q]
    return jnp.concatenate([-w2, w1, -w4, w3], axis=-1)


def _pad_lanes(w, width=LANES):
    return jnp.pad(w, [(0, 0)] * (w.ndim - 1) + [(0, width - w.shape[-1])])


def _pack_layer(l, w_in, b_in, q_norm, kv_norm, w_uq, w_uk, w_uv, w_oa, conv_w, conv_b, w_ob, w_o,
                ln1_g, ln1_b, ln2_g, ln2_b, w_router, b_router, w_gu, b_gu, w_down, b_down):
    wi, bi = w_in[l], b_in[l][None, :]

    def pack(m):
        kr = m[:, OFF_KR:OFF_CX]
        return jnp.concatenate([
            m[:, OFF_QA:OFF_KR], _pad_lanes(kr), _pad_lanes(_rot_cols(kr)),
            m[:, OFF_CX:OFF_GA], m[:, OFF_GA:IN_WIDTH]], axis=1)

    uq = w_uq[l].reshape(Q_LORA, N_HEADS, QK_NOPE + QK_ROPE)
    uq_nope = uq[:, :, :QK_NOPE].reshape(Q_LORA, N_HEADS * QK_NOPE)
    uq_rope = uq[:, :, QK_NOPE:]
    wq = jnp.concatenate([
        uq_nope,
        _pad_lanes(uq_rope).reshape(Q_LORA, N_HEADS * LANES),
        _pad_lanes(_rot_cols(uq_rope)).reshape(Q_LORA, N_HEADS * LANES)], axis=1)

    eye = jnp.eye(N_HEADS, dtype=F32)
    uk = w_uk[l].reshape(KV_LORA, N_HEADS, QK_NOPE)
    bduk = jnp.einsum("chd,hg->hdgc", uk, eye).reshape(N_HEADS * QK_NOPE, N_HEADS * KV_LORA)
    uv = w_uv[l].reshape(KV_LORA, N_HEADS, V_DIM)
    bduv = jnp.einsum("chd,hg->hcgd", uv, eye).reshape(N_HEADS * KV_LORA, N_HEADS * V_DIM)

    return dict(
        w_pack=pack(wi).astype(BF16), b_pack=pack(bi),
        q_norm=q_norm[l][None, :], kv_norm=kv_norm[l][None, :],
        wq=wq.astype(BF16), bduk=bduk.astype(BF16), bduv=bduv.astype(BF16),
        w_oa=w_oa[l].astype(BF16), w_ob=w_ob[l].astype(BF16), w_o=w_o[l].astype(BF16),
        conv_w=jnp.pad(conv_w[l], ((0, 8 - CONV_K), (0, 0))), conv_b=conv_b[l][None, :],
        ln1_g=ln1_g[l][None, :], ln1_b=ln1_b[l][None, :],
        ln2_g=ln2_g[l][None, :], ln2_b=ln2_b[l][None, :],
        w_router=_pad_lanes(w_router[l]),
        b_router=jnp.pad(b_router[l], (0, LANES - N_EXPERTS), constant_values=NEG_BIG)[None, :],
        w_gu=w_gu[l].astype(BF16), b_gu=b_gu[l], w_down=w_down[l].astype(BF16), b_down=b_down[l],
    )


def _rope_tables(n):
    n_rows = n // GRID_W
    row = jnp.broadcast_to(jnp.arange(n_rows, dtype=F32)[:, None], (n_rows, GRID_W)).reshape(-1)
    col = jnp.broadcast_to(jnp.arange(GRID_W, dtype=F32)[None, :], (n_rows, GRID_W)).reshape(-1)
    half = QK_ROPE // 2
    freqs = ROPE_THETA ** (-jnp.arange(0, half, 2, dtype=F32) / half)
    ang_r = row[:, None] * freqs
    ang_c = col[:, None] * freqs
    ang = jnp.concatenate([ang_r, ang_r, ang_c, ang_c], axis=-1)
    return _pad_lanes(jnp.cos(ang)), _pad_lanes(jnp.sin(ang))


def kernel(x, c, ctx, c_ctx, w_ada, b_ada, w_in, b_in, q_norm, kv_norm, w_uq, w_uk, w_uv, w_oa,
           conv_w, conv_b, w_ob, w_o, ln1_g, ln1_b, ln2_g, ln2_b, w_router, b_router,
           w_gu, b_gu, w_down, b_down):
    b, n, d = x.shape
    nc = ctx.shape[1]
    cvec = jnp.concatenate([c, c_ctx[None, :], jnp.zeros((8 - b - 1, d), F32)], axis=0)
    mod = _modulation(cvec, w_ada, b_ada)

    cos_x, sin_x = _rope_tables(n)
    cos_c = _pad_lanes(jnp.ones((nc, QK_ROPE), F32))
    sin_c = jnp.zeros((nc, LANES), F32)

    for l in range(DEPTH):
        update_ctx = l < DEPTH - 1
        lw = _pack_layer(l, w_in, b_in, q_norm, kv_norm, w_uq, w_uk, w_uv, w_oa, conv_w, conv_b,
                         w_ob, w_o, ln1_g, ln1_b, ln2_g, ln2_b, w_router, b_router,
                         w_gu, b_gu, w_down, b_down)
        sh1, sc1, g1, sh2, sc2, g2 = [m[:, None, :] for m in jnp.split(mod[l], 6, axis=-1)]
        xm = lambda m: m[:b]
        cm = lambda m: jnp.broadcast_to(m[b:b + 1], (b, 1, d))

        q_x, kt_x, v_x, zc_x, bg_x, gs_x = _in_projection(
            x, 1.0 + xm(sc1), xm(sh1), cos_x, sin_x, lw, ROW_TILE)
        q_c, kt_c, v_c, zc_c, bg_c, gs_c = _in_projection(
            ctx, 1.0 + cm(sc1), cm(sh1), cos_c, sin_c, lw, nc)
        kt_all = jnp.concatenate([kt_c, kt_x], axis=2)
        v_all = jnp.concatenate([v_c, v_x], axis=1)
        o_x = _latent_attention(q_x, kt_all, v_all, Q_TILE, KV_CHUNK)
        x1, h2, ridx, rgate = _merge_ln_route(
            o_x, zc_x, bg_x, gs_x, x, xm(g1), 1.0 + xm(sc2), xm(sh2), lw, ROW_TILE)
        tokens_h = h2.reshape(b * n, d)
        ridx = ridx.reshape(b * n, LANES)[:, :TOP_K]
        rgate = rgate.reshape(b * n, LANES)[:, :TOP_K]
        if update_ctx:
            o_c = _latent_attention(q_c, kt_c, v_c, nc, nc)
            c1, h2c, ridx_c, rgate_c = _merge_ln_route(
                o_c, zc_c, bg_c, gs_c, ctx, cm(g1), 1.0 + cm(sc2), cm(sh2), lw, nc)
            tokens_h = jnp.concatenate([tokens_h, h2c.reshape(b * nc, d)], axis=0)
            ridx = jnp.concatenate([ridx, ridx_c.reshape(b * nc, LANES)[:, :TOP_K]], axis=0)
            rgate = jnp.concatenate([rgate, rgate_c.reshape(b * nc, LANES)[:, :TOP_K]], axis=0)

        pos, src, gsort, te, tv = _routing_tables(ridx, rgate, EXPERT_TILE)
        xs = _dispatch(src, tokens_h, EXPERT_TILE)
        ys = _expert_ffn(te, tv, xs, gsort, lw["w_gu"], lw["b_gu"], lw["w_down"], lw["b_down"],
                         EXPERT_TILE)
        x = _combine_ln(_pos_tiles(pos[:b * n], COMBINE_TILE), ys, x1.reshape(b * n, d), xm(g2), n,
                        lw["ln2_g"], lw["ln2_b"], COMBINE_TILE).reshape(b, n, d)
        if update_ctx:
            ctx = _combine_ln(_pos_tiles(pos[b * n:], COMBINE_TILE), ys, c1.reshape(b * nc, d),
                              cm(g2), nc, lw["ln2_g"], lw["ln2_b"], COMBINE_TILE).reshape(b, nc, d)
    return x
```

```python
import functools

import jax
import jax.numpy as jnp
from jax import lax
from jax.experimental import pallas as pl
from jax.experimental.pallas import tpu as pltpu

F32 = jnp.float32
BF16 = jnp.bfloat16
I32 = jnp.int32
U32 = jnp.uint32

D_MODEL = 1024
DEPTH = 2
GRID_W = 64
N_HEADS = 8
QK_NOPE = 64
QK_ROPE = 32
V_DIM = 64
Q_LORA = 256
KV_LORA = 128
ROPE_THETA = 10000.0
ATTN_SCALE = (QK_NOPE + QK_ROPE) ** -0.5
CONV_WIDTH = 512
CONV_K = 3
N_EXPERTS = 32
TOP_K = 4
D_FF = D_MODEL
SWIGLU_LIMIT = 7.0
SWIGLU_ALPHA = 1.702
NORM_EPS = 1e-6
DEEPNORM_ALPHA = (2 * DEPTH) ** 0.25

OFF_QA = 0
OFF_KV = OFF_QA + Q_LORA
OFF_KR = OFF_KV + KV_LORA
OFF_CX = OFF_KR + QK_ROPE
OFF_CB = OFF_CX + CONV_WIDTH
OFF_CC = OFF_CB + CONV_WIDTH
OFF_GA = OFF_CC + CONV_WIDTH
OFF_GB = OFF_GA + D_MODEL
IN_WIDTH = OFF_GB + D_MODEL

LANES = 128
QK_WIDTH = 2 * LANES
LOG2E = 1.4426950408889634
NEG_BIG = -1e30
VMEM_LIMIT = 56 * 1024 * 1024

PA_W = Q_LORA + KV_LORA + 2 * LANES
PC_OFF = PA_W
PC_W = 3 * CONV_WIDTH
PG_OFF = PC_OFF + PC_W
PG_W = 2 * D_MODEL
PACK_W = PG_OFF + PG_W

ROW_TILE = 512
Q_TILE = 1024
KV_CHUNK = 768
EXPERT_TILE = 512
BLOCK_TOKENS = 512
XROW_W = D_MODEL + LANES
LOCAL_ROWS = TOP_K * BLOCK_TOKENS + N_EXPERTS * 8
STRIP_BITS = 9


def _dot(a, b):
    return jnp.dot(a, b, preferred_element_type=F32)


def _split_bf16(a):
    hi = a.astype(BF16)
    lo = (a - hi.astype(F32)).astype(BF16)
    return hi, lo


def _dot3(a, b):
    ah, al = _split_bf16(a)
    bh, bl = _split_bf16(b)
    return _dot(ah, bh) + (_dot(ah, bl) + _dot(al, bh))


def _rms_norm(x, g):
    return x * lax.rsqrt(jnp.mean(jnp.square(x), axis=-1, keepdims=True) + NORM_EPS) * g


def _layer_norm(x, g, b):
    mu = jnp.mean(x, axis=-1, keepdims=True)
    xc = x - mu
    var = jnp.mean(jnp.square(xc), axis=-1, keepdims=True)
    return xc * lax.rsqrt(var + NORM_EPS) * g + b


def _const_spec(shape):
    nd = len(shape)
    return pl.BlockSpec(shape, lambda *_: (0,) * nd, pipeline_mode=pl.Buffered(1))


def _mod_kernel(c_ref, w_ref, b_ref, o_ref):
    c = c_ref[...]
    s = c * jax.nn.sigmoid(c)
    o_ref[0] = _dot3(s, w_ref[0]) + b_ref[0]


def _modulation(cvec, w_ada, b_ada):
    depth, d, n = w_ada.shape
    tn = 1536
    return pl.pallas_call(
        _mod_kernel,
        out_shape=jax.ShapeDtypeStruct((depth, 8, n), F32),
        grid=(depth, n // tn),
        in_specs=[
            pl.BlockSpec((8, d), lambda l, j: (0, 0)),
            pl.BlockSpec((1, d, tn), lambda l, j: (l, 0, j)),
            pl.BlockSpec((1, 1, tn), lambda l, j: (l, 0, j)),
        ],
        out_specs=pl.BlockSpec((1, 8, tn), lambda l, j: (l, 0, j)),
        compiler_params=pltpu.CompilerParams(
            dimension_semantics=("arbitrary", "arbitrary"), vmem_limit_bytes=VMEM_LIMIT),
        name="adaln_mod",
    )(cvec, w_ada, b_ada.reshape(depth, 1, n))


def _inproj_kernel(x_ref, scp_ref, sh_ref, cos_ref, sin_ref, w_ref, b_ref, qn_ref, kvn_ref,
                   wq_ref, bduk_ref,
                   q_ref, kt_ref, v_ref, zc_ref, bg_ref, gs_ref):
    h = (x_ref[0] * scp_ref[0] + sh_ref[0]).astype(BF16)
    cos = cos_ref[...]
    sin = sin_ref[...]

    pa = _dot(h, w_ref[:, 0:PA_W]) + b_ref[:, 0:PA_W]
    qn = _rms_norm(pa[:, 0:Q_LORA], qn_ref[...]).astype(BF16)
    ckv = _rms_norm(pa[:, Q_LORA:Q_LORA + KV_LORA], kvn_ref[...])
    kr = pa[:, Q_LORA + KV_LORA:Q_LORA + KV_LORA + LANES]
    krr = pa[:, Q_LORA + KV_LORA + LANES:PA_W]
    k_rope = kr * cos + krr * sin
    k_row = jnp.concatenate([ckv, k_rope], axis=1)
    kt_ref[0] = k_row.T.astype(BF16)
    v_ref[0] = ckv.astype(BF16)

    nope_w = N_HEADS * QK_NOPE
    blk_w = N_HEADS * LANES
    q_all = _dot(qn, wq_ref[...])
    q_lat = _dot(q_all[:, 0:nope_w].astype(BF16), bduk_ref[...])
    qscale = ATTN_SCALE * LOG2E
    for hd in range(N_HEADS):
        lo = hd * LANES
        qr = q_all[:, nope_w + lo:nope_w + lo + LANES]
        qrr = q_all[:, nope_w + blk_w + lo:nope_w + blk_w + lo + LANES]
        q_rope = qr * cos + qrr * sin
        q_row = jnp.concatenate([q_lat[:, lo:lo + LANES], q_rope], axis=1) * qscale
        q_ref[0, hd] = q_row.astype(BF16)

    pc = _dot(h, w_ref[:, PC_OFF:PC_OFF + PC_W]) + b_ref[:, PC_OFF:PC_OFF + PC_W]
    zc_ref[0] = pc[:, 2 * CONV_WIDTH:3 * CONV_WIDTH] * pc[:, 0:CONV_WIDTH]
    bg_ref[0] = pc[:, CONV_WIDTH:2 * CONV_WIDTH]

    pg = _dot(h, w_ref[:, PG_OFF:PG_OFF + PG_W]) + b_ref[:, PG_OFF:PG_OFF + PG_W]
    gs_ref[0] = jax.nn.sigmoid(pg).astype(BF16)


def _in_projection(x, scp, sh, cos, sin, lw, tm):
    b, s, d = x.shape
    row = lambda bi, i: (bi, i, 0)
    per_b = lambda bi, i: (bi, 0, 0)
    return pl.pallas_call(
        _inproj_kernel,
        out_shape=(
            jax.ShapeDtypeStruct((b, N_HEADS, s, QK_WIDTH), BF16),
            jax.ShapeDtypeStruct((b, QK_WIDTH, s), BF16),
            jax.ShapeDtypeStruct((b, s, KV_LORA), BF16),
            jax.ShapeDtypeStruct((b, s, CONV_WIDTH), F32),
            jax.ShapeDtypeStruct((b, s, CONV_WIDTH), F32),
            jax.ShapeDtypeStruct((b, s, 2 * D_MODEL), BF16),
        ),
        grid=(b, s // tm),
        in_specs=[
            pl.BlockSpec((1, tm, d), row),
            pl.BlockSpec((1, 1, d), per_b),
            pl.BlockSpec((1, 1, d), per_b),
            pl.BlockSpec((tm, LANES), lambda bi, i: (i, 0)),
            pl.BlockSpec((tm, LANES), lambda bi, i: (i, 0)),
            _const_spec((d, PACK_W)),
            _const_spec((1, PACK_W)),
            _const_spec((1, Q_LORA)),
            _const_spec((1, KV_LORA)),
            _const_spec(lw["wq"].shape),
            _const_spec(lw["bduk"].shape),
        ],
        out_specs=(
            pl.BlockSpec((1, N_HEADS, tm, QK_WIDTH), lambda bi, i: (bi, 0, i, 0)),
            pl.BlockSpec((1, QK_WIDTH, tm), lambda bi, i: (bi, 0, i)),
            pl.BlockSpec((1, tm, KV_LORA), row),
            pl.BlockSpec((1, tm, CONV_WIDTH), row),
            pl.BlockSpec((1, tm, CONV_WIDTH), row),
            pl.BlockSpec((1, tm, 2 * D_MODEL), row),
        ),
        compiler_params=pltpu.CompilerParams(
            dimension_semantics=("arbitrary", "arbitrary"), vmem_limit_bytes=VMEM_LIMIT),
        name="in_projection",
    )(x, scp, sh, cos, sin, lw["w_pack"], lw["b_pack"], lw["q_norm"], lw["kv_norm"],
      lw["wq"], lw["bduk"])


def _attn_kernel(q_ref, kt_ref, v_ref, o_ref, *, tk, nchunks):
    q = q_ref[0, 0]
    tq = q.shape[0]

    def body(c, carry):
        m, l, acc = carry
        off = pl.multiple_of(c * tk, tk)
        s = _dot(q, kt_ref[0, :, pl.ds(off, tk)])
        m_new = jnp.maximum(m, jnp.max(s, axis=-1, keepdims=True))
        p = jnp.exp2(s - m_new)
        a = jnp.exp2(m - m_new)
        l = a * l + jnp.sum(p, axis=-1, keepdims=True)
        acc = a * acc + _dot(p.astype(BF16), v_ref[0, pl.ds(off, tk), :])
        return m_new, l, acc

    init = (jnp.full((tq, 1), NEG_BIG, F32), jnp.zeros((tq, 1), F32), jnp.zeros((tq, KV_LORA), F32))
    _, l, acc = lax.fori_loop(0, nchunks, body, init)
    o_ref[0] = (acc / l).astype(BF16)


def _latent_attention(q, kt, v, tq, tk):
    b, nh, s, _ = q.shape
    skv = kt.shape[2]
    return pl.pallas_call(
        functools.partial(_attn_kernel, tk=tk, nchunks=skv // tk),
        out_shape=jax.ShapeDtypeStruct((b, s, nh * KV_LORA), BF16),
        grid=(b, nh, s // tq),
        in_specs=[
            pl.BlockSpec((1, 1, tq, QK_WIDTH), lambda bi, hi, i: (bi, hi, i, 0)),
            pl.BlockSpec((1, QK_WIDTH, skv), lambda bi, hi, i: (bi, 0, 0)),
            pl.BlockSpec((1, skv, KV_LORA), lambda bi, hi, i: (bi, 0, 0)),
        ],
        out_specs=pl.BlockSpec((1, tq, KV_LORA), lambda bi, hi, i: (bi, i, hi)),
        compiler_params=pltpu.CompilerParams(
            dimension_semantics=("arbitrary", "arbitrary", "arbitrary"),
            vmem_limit_bytes=VMEM_LIMIT),
        name="latent_attention",
    )(q, kt, v)


def _merge_kernel(o_ref, zc_ref, zp_ref, zn_ref, bg_ref, gs_ref, x_ref, g1_ref, scp2_ref, sh2_ref,
                  bduv_ref, woa_ref, wob_ref, wo_ref, cw_ref, cb_ref, lng_ref, lnb_ref,
                  wr_ref, br_ref,
                  x1_ref, hx_ref, ridx_ref, ridxt_ref, *, ntiles):
    i = pl.program_id(1)
    tm = x_ref.shape[1]

    attn = _dot(o_ref[0], bduv_ref[...]).astype(BF16)
    y_a = _dot(attn, woa_ref[...])

    zc = zc_ref[0]
    rows = lax.broadcasted_iota(I32, zc.shape, 0)
    prev_row = jnp.where(i > 0, zp_ref[0][7:8, :], 0.0)
    next_row = jnp.where(i < ntiles - 1, zn_ref[0][0:1, :], 0.0)
    z_prev = jnp.where(rows == 0, prev_row, pltpu.roll(zc, 1, axis=0))
    z_next = jnp.where(rows == tm - 1, next_row, pltpu.roll(zc, tm - 1, axis=0))
    cw = cw_ref[...]
    conv = bg_ref[0] * (z_prev * cw[0:1] + zc * cw[1:2] + z_next * cw[2:3] + cb_ref[...])
    y_b = _dot(conv.astype(BF16), wob_ref[...])

    gs = gs_ref[0].astype(F32)
    merged = gs[:, 0:D_MODEL] * y_a + gs[:, D_MODEL:2 * D_MODEL] * y_b
    y = _dot(merged.astype(BF16), wo_ref[...])

    x1 = _layer_norm(DEEPNORM_ALPHA * x_ref[0] + g1_ref[0] * y, lng_ref[...], lnb_ref[...])
    x1_ref[0] = x1
    h2 = x1 * scp2_ref[0] + sh2_ref[0]

    logits = _dot3(h2, wr_ref[...]) + br_ref[...]
    lanes = lax.broadcasted_iota(I32, logits.shape, 1).astype(F32)
    sels, exps = [], []
    top = None
    denom = None
    for k in range(TOP_K):
        mx = jnp.max(logits, axis=-1, keepdims=True)
        sel = jnp.min(jnp.where(logits == mx, lanes, float(LANES)), axis=-1, keepdims=True)
        if k == 0:
            top = mx
            e = jnp.ones_like(mx)
            denom = e
        else:
            e = jnp.exp(mx - top)
            denom = denom + e
        sels.append(sel)
        exps.append(e)
        logits = jnp.where(lanes == sel, NEG_BIG * 2.0, logits)

    idx_out = jnp.zeros(logits.shape, F32)
    meta = jnp.zeros(logits.shape, F32)
    for k in range(TOP_K):
        g = exps[k] / denom
        hi = g.astype(BF16).astype(F32)
        r1 = g - hi
        mid = r1.astype(BF16).astype(F32)
        lo = (r1 - mid).astype(BF16).astype(F32)
        idx_out = jnp.where(lanes == k, sels[k], idx_out)
        meta = jnp.where(lanes == TOP_K + k, hi, meta)
        meta = jnp.where(lanes == 2 * TOP_K + k, mid, meta)
        meta = jnp.where(lanes == 3 * TOP_K + k, lo, meta)
    meta = meta + idx_out
    ridx_ref[0] = idx_out.astype(I32)
    ridxt_ref[0] = idx_out.T[0:8, :].astype(I32)
    hx_ref[0] = jnp.concatenate([h2.astype(BF16), meta.astype(BF16)], axis=1)


def _merge_ln_route(o, zc, bg, gs, x, g1, scp2, sh2, lw, tm):
    b, s, d = x.shape
    ntiles = s // tm
    h8 = tm // 8
    row = lambda bi, i: (bi, i, 0)
    per_b = lambda bi, i: (bi, 0, 0)
    consts = [lw["bduv"], lw["w_oa"], lw["w_ob"], lw["w_o"], lw["conv_w"], lw["conv_b"],
              lw["ln1_g"], lw["ln1_b"], lw["w_router"], lw["b_router"]]
    return pl.pallas_call(
        functools.partial(_merge_kernel, ntiles=ntiles),
        out_shape=(
            jax.ShapeDtypeStruct((b, s, d), F32),
            jax.ShapeDtypeStruct((b, s, XROW_W), BF16),
            jax.ShapeDtypeStruct((b, s, LANES), I32),
            jax.ShapeDtypeStruct((b * ntiles, 8, tm), I32),
        ),
        grid=(b, ntiles),
        in_specs=[
            pl.BlockSpec((1, tm, N_HEADS * KV_LORA), row),
            pl.BlockSpec((1, tm, CONV_WIDTH), row),
            pl.BlockSpec((1, 8, CONV_WIDTH), lambda bi, i: (bi, jnp.maximum(i * h8 - 1, 0), 0)),
            pl.BlockSpec((1, 8, CONV_WIDTH),
                         lambda bi, i: (bi, jnp.minimum((i + 1) * h8, s // 8 - 1), 0)),
            pl.BlockSpec((1, tm, CONV_WIDTH), row),
            pl.BlockSpec((1, tm, 2 * D_MODEL), row),
            pl.BlockSpec((1, tm, d), row),
            pl.BlockSpec((1, 1, d), per_b),
            pl.BlockSpec((1, 1, d), per_b),
            pl.BlockSpec((1, 1, d), per_b),
        ] + [_const_spec(w.shape) for w in consts],
        out_specs=(
            pl.BlockSpec((1, tm, d), row),
            pl.BlockSpec((1, tm, XROW_W), row),
            pl.BlockSpec((1, tm, LANES), row),
            pl.BlockSpec((1, 8, tm), lambda bi, i: (bi * ntiles + i, 0, 0)),
        ),
        compiler_params=pltpu.CompilerParams(
            dimension_semantics=("arbitrary", "arbitrary"), vmem_limit_bytes=VMEM_LIMIT),
        name="merge_ln_route",
    )(o, zc, zc, zc, bg, gs, x, g1, scp2, sh2, *consts)


def _strip_copies(n8, src_ref, src_row, dst_ref, dst_row, sem):
    for bit in reversed(range(STRIP_BITS)):
        done = ((n8 >> (bit + 1)) << (bit + 1)) * 8

        @pl.when(((n8 >> bit) & 1) == 1)
        def _(bit=bit, done=done):
            s = pl.multiple_of(src_row + done, 8)
            d = pl.multiple_of(dst_row + done, 8)
            pltpu.make_async_copy(src_ref.at[pl.ds(s, 8 << bit)], dst_ref.at[pl.ds(d, 8 << bit)],
                                  sem).start()


def _dispatch_kernel(goff_ref, n8_ref, lst_ref, tot_ref, ridxt_ref, lcol_ref, hx_ref, zeros_hbm,
                     xs_hbm, buf, sem, *, dump_row):
    del zeros_hbm
    b = pl.program_id(0)
    tb = hx_ref.shape[0]
    ridxt = ridxt_ref[0]
    e_iota = lax.broadcasted_iota(I32, (N_EXPERTS, tb), 0)
    hits = [ridxt[k:k + 1, :] == e_iota for k in range(TOP_K)]
    mt = sum(jnp.where(h, 1.0, 0.0) for h in hits)
    earlier = lax.broadcasted_iota(I32, (tb, tb), 0) < lax.broadcasted_iota(I32, (tb, tb), 1)
    rank = _dot(mt.astype(BF16), jnp.where(earlier, 1.0, 0.0).astype(BF16))
    base = lcol_ref[0] + rank
    slot = [jnp.sum(jnp.where(h, base, 0.0), axis=0, keepdims=True) for h in hits]

    j = lax.broadcasted_iota(I32, (LOCAL_ROWS, tb), 0).astype(F32)
    sel = jnp.where(j == slot[0], 1.0,
                    jnp.where(j == slot[1], 1.0,
                              jnp.where(j == slot[2], 1.0, jnp.where(j == slot[3], 1.0, 0.0))))
    buf[...] = _dot(sel.astype(BF16), hx_ref[...])

    for e in range(N_EXPERTS):
        t = b * N_EXPERTS + e
        _strip_copies(n8_ref[t], buf, lst_ref[t], xs_hbm, goff_ref[t], sem)
    tot8 = tot_ref[b]
    _strip_copies(LOCAL_ROWS // 8 - tot8, buf, tot8 * 8, xs_hbm, dump_row + tot8 * 8, sem)
    pltpu.make_async_copy(buf, xs_hbm.at[pl.ds(0, LOCAL_ROWS)], sem).wait()


def _dispatch(tabs, ridxt, hx, nslots):
    t, w = hx.shape
    tb = BLOCK_TOKENS
    nb = t // tb
    grid_spec = pltpu.PrefetchScalarGridSpec(
        num_scalar_prefetch=4,
        grid=(nb,),
        in_specs=[
            pl.BlockSpec((1, 8, tb), lambda i, *_: (i, 0, 0)),
            pl.BlockSpec((1, N_EXPERTS, 1), lambda i, *_: (i, 0, 0)),
            pl.BlockSpec((tb, w), lambda i, *_: (i, 0)),
            pl.BlockSpec(memory_space=pl.ANY),
        ],
        out_specs=pl.BlockSpec(memory_space=pl.ANY),
        scratch_shapes=[pltpu.VMEM((LOCAL_ROWS, w), F32), pltpu.SemaphoreType.DMA(())],
    )
    out_shape = jax.ShapeDtypeStruct((nslots + LOCAL_ROWS, w), F32)
    return pl.pallas_call(
        functools.partial(_dispatch_kernel, dump_row=nslots),
        out_shape=out_shape,
        grid_spec=grid_spec,
        input_output_aliases={7: 0},
        compiler_params=pltpu.CompilerParams(
            dimension_semantics=("arbitrary",), vmem_limit_bytes=VMEM_LIMIT),
        name="moe_dispatch",
    )(tabs["goff"], tabs["n8"], tabs["lst"], tabs["tot8"], ridxt, tabs["lcol"], hx,
      jnp.zeros(out_shape.shape, out_shape.dtype))


def _expert_kernel(te_ref, nv_ref, x_ref, wgu_ref, bgu_ref, wd_ref, bd_ref, y_ref):
    j = pl.program_id(0)
    nv = nv_ref[j]

    @pl.when(nv > 0)
    def _():
        meta = x_ref[:, D_MODEL:XROW_W]
        lanes = lax.broadcasted_iota(I32, meta.shape, 1)
        gates = (pltpu.roll(meta, LANES - TOP_K, axis=1) + pltpu.roll(meta, LANES - 2 * TOP_K, axis=1)
                 + pltpu.roll(meta, LANES - 3 * TOP_K, axis=1))
        mine = (lanes < TOP_K) & (meta == te_ref[j].astype(F32))
        g = jnp.sum(jnp.where(mine, gates, 0.0), axis=-1, keepdims=True)

        xb = x_ref[:, 0:D_MODEL].astype(BF16)
        gu = _dot(xb, wgu_ref[0]) + bgu_ref[0]
        gate = jnp.minimum(gu[:, 0:D_FF], SWIGLU_LIMIT)
        up = jnp.clip(gu[:, D_FF:2 * D_FF], -SWIGLU_LIMIT, SWIGLU_LIMIT)
        act = (up + 1.0) * gate * jax.nn.sigmoid(SWIGLU_ALPHA * gate)
        y = g * (_dot(act.astype(BF16), wd_ref[0]) + bd_ref[0])
        half = D_MODEL // 2
        lo = lax.bitcast_convert_type(y[:, 0:half].astype(BF16).astype(F32), U32)
        hi = lax.bitcast_convert_type(y[:, half:D_MODEL].astype(BF16).astype(F32), U32)
        y_ref[...] = (lo >> 16) | (hi & jnp.uint32(0xFFFF0000))

    @pl.when(nv <= 0)
    def _():
        y_ref[...] = jnp.zeros_like(y_ref)


def _expert_ffn(tabs, xs, wgu, bgu, wd, bd, ntiles):
    tme = EXPERT_TILE
    w = xs.shape[1]
    ne, d, f2 = wgu.shape
    f = wd.shape[1]
    grid_spec = pltpu.PrefetchScalarGridSpec(
        num_scalar_prefetch=2,
        grid=(ntiles,),
        in_specs=[
            pl.BlockSpec((tme, w), lambda j, te, nv: (j, 0)),
            pl.BlockSpec((1, d, f2), lambda j, te, nv: (te[j], 0, 0)),
            pl.BlockSpec((1, 1, f2), lambda j, te, nv: (te[j], 0, 0)),
            pl.BlockSpec((1, f, d), lambda j, te, nv: (te[j], 0, 0)),
            pl.BlockSpec((1, 1, d), lambda j, te, nv: (te[j], 0, 0)),
        ],
        out_specs=pl.BlockSpec((tme, d // 2), lambda j, te, nv: (j, 0)),
    )
    return pl.pallas_call(
        _expert_kernel,
        out_shape=jax.ShapeDtypeStruct((ntiles * tme, d // 2), U32),
        grid_spec=grid_spec,
        compiler_params=pltpu.CompilerParams(
            dimension_semantics=("arbitrary",), vmem_limit_bytes=VMEM_LIMIT),
        name="expert_ffn",
    )(tabs["te"], tabs["nv"], xs, wgu, bgu.reshape(ne, 1, f2), wd, bd.reshape(ne, 1, d))


def _combine_kernel(goff_ref, n8_ref, lst_ref, tot_ref, ys_hbm, ridx_ref, lrow_ref, x1_ref, g2_ref,
                    lng_ref, lnb_ref, o_ref, ybuf, sem, *, block0):
    b = pl.program_id(0) + block0
    tb = x1_ref.shape[0]
    for e in range(N_EXPERTS):
        t = b * N_EXPERTS + e
        _strip_copies(n8_ref[t], ys_hbm, goff_ref[t], ybuf, lst_ref[t], sem)
    tot8 = tot_ref[b]
    _strip_copies(LOCAL_ROWS // 8 - tot8, ys_hbm, 0, ybuf, tot8 * 8, sem)

    ridx = ridx_ref[...]
    lanes = lax.broadcasted_iota(I32, ridx.shape, 1)
    hits = [ridx[:, k:k + 1] == lanes for k in range(TOP_K)]
    m = sum(jnp.where(h, 1.0, 0.0) for h in hits)
    earlier = lax.broadcasted_iota(I32, (tb, tb), 1) < lax.broadcasted_iota(I32, (tb, tb), 0)
    rank = _dot(jnp.where(earlier, 1.0, 0.0).astype(BF16), m.astype(BF16))
    base = lrow_ref[0] + rank
    slot = [jnp.sum(jnp.where(h, base, 0.0), axis=-1, keepdims=True) for h in hits]
    j = lax.broadcasted_iota(I32, (tb, LOCAL_ROWS), 1).astype(F32)
    sel = jnp.where(j == slot[0], 1.0,
                    jnp.where(j == slot[1], 1.0,
                              jnp.where(j == slot[2], 1.0, jnp.where(j == slot[3], 1.0, 0.0))))
    sel = sel.astype(BF16)

    pltpu.make_async_copy(ys_hbm.at[pl.ds(0, LOCAL_ROWS)], ybuf, sem).wait()
    u = ybuf[...]
    ylo = lax.bitcast_convert_type(u << 16, F32).astype(BF16)
    yhi = lax.bitcast_convert_type(u & jnp.uint32(0xFFFF0000), F32).astype(BF16)
    f = jnp.concatenate([_dot(sel, ylo), _dot(sel, yhi)], axis=1)
    r = DEEPNORM_ALPHA * x1_ref[...] + g2_ref[0] * f
    o_ref[...] = _layer_norm(r, lng_ref[...], lnb_ref[...])


def _combine_ln(tabs, ys, ridx_all, x1, g2, rows_per_mod, ln_g, ln_b, block0):
    t, d = x1.shape
    tb = BLOCK_TOKENS
    grid_spec = pltpu.PrefetchScalarGridSpec(
        num_scalar_prefetch=4,
        grid=(t // tb,),
        in_specs=[
            pl.BlockSpec(memory_space=pl.ANY),
            pl.BlockSpec((tb, LANES), lambda i, *_: (i + block0, 0)),
            pl.BlockSpec((1, 1, LANES), lambda i, *_: (i + block0, 0, 0)),
            pl.BlockSpec((tb, d), lambda i, *_: (i, 0)),
            pl.BlockSpec((1, 1, d), lambda i, *_: ((i * tb) // rows_per_mod, 0, 0)),
            pl.BlockSpec((1, d), lambda i, *_: (0, 0)),
            pl.BlockSpec((1, d), lambda i, *_: (0, 0)),
        ],
        out_specs=pl.BlockSpec((tb, d), lambda i, *_: (i, 0)),
        scratch_shapes=[pltpu.VMEM((LOCAL_ROWS, d // 2), U32), pltpu.SemaphoreType.DMA(())],
    )
    return pl.pallas_call(
        functools.partial(_combine_kernel, block0=block0),
        out_shape=jax.ShapeDtypeStruct((t, d), F32),
        grid_spec=grid_spec,
        compiler_params=pltpu.CompilerParams(
            dimension_semantics=("arbitrary",), vmem_limit_bytes=VMEM_LIMIT),
        name="moe_combine_ln",
    )(tabs["goff"], tabs["n8"], tabs["lst"], tabs["tot8"], ys, ridx_all, tabs["lrow"], x1, g2,
      ln_g, ln_b)


def _num_expert_tiles(t):
    nb = t // BLOCK_TOKENS
    return -(-(TOP_K * t + nb * N_EXPERTS * 7) // EXPERT_TILE) + N_EXPERTS


def _route_tables(ridx_all):
    t = ridx_all.shape[0]
    tb, tme = BLOCK_TOKENS, EXPERT_TILE
    nb = t // tb
    experts = jnp.arange(N_EXPERTS, dtype=I32)
    onehot = jnp.sum((ridx_all[:, :TOP_K, None] == experts[None, None, :]).astype(I32), axis=1)
    n8 = (jnp.sum(onehot.reshape(nb, tb, N_EXPERTS), axis=1) + 7) // 8
    lst8 = jnp.cumsum(n8, axis=1) - n8
    tot8 = jnp.sum(n8, axis=1)
    g8 = jnp.sum(n8, axis=0)
    t8 = tme // 8
    tiles = (g8 + t8 - 1) // t8
    tile_end = jnp.cumsum(tiles)
    tile_first = tile_end - tiles
    goff8 = (tile_first * t8)[None, :] + jnp.cumsum(n8, axis=0) - n8
    ntiles = _num_expert_tiles(t)
    jt = jnp.arange(ntiles, dtype=I32)
    te = jnp.minimum(jnp.sum((jt[:, None] >= tile_end[None, :]).astype(I32), axis=1), N_EXPERTS - 1)
    nv = jnp.clip(g8[te] * 8 - (jt - tile_first[te]) * tme, 0, tme)
    nv = jnp.where(jt < tile_end[-1], nv, 0)
    lst = (lst8 * 8).astype(I32)
    return dict(
        goff=(goff8 * 8).astype(I32).reshape(-1), n8=n8.astype(I32).reshape(-1), lst=lst.reshape(-1),
        tot8=tot8.astype(I32), te=te.astype(I32), nv=nv.astype(I32),
        lcol=lst.astype(F32)[:, :, None],
        lrow=_pad_lanes(lst.astype(F32))[:, None, :],
    )


def _rot_cols(w):
    q = QK_ROPE // 4
    w1, w2, w3, w4 = w[..., 0:q], w[..., q:2 * q], w[..., 2 * q:3 * q], w[..., 3 * q:4 * q]
    return jnp.concatenate([-w2, w1, -w4, w3], axis=-1)


def _pad_lanes(w, width=LANES):
    return jnp.pad(w, [(0, 0)] * (w.ndim - 1) + [(0, width - w.shape[-1])])


def _pack_layer(l, w_in, b_in, q_norm, kv_norm, w_uq, w_uk, w_uv, w_oa, conv_w, conv_b, w_ob, w_o,
                ln1_g, ln1_b, ln2_g, ln2_b, w_router, b_router, w_gu, b_gu, w_down, b_down):
    wi, bi = w_in[l], b_in[l][None, :]

    def pack(m):
        kr = m[:, OFF_KR:OFF_CX]
        return jnp.concatenate([
            m[:, OFF_QA:OFF_KR], _pad_lanes(kr), _pad_lanes(_rot_cols(kr)),
            m[:, OFF_CX:OFF_GA], m[:, OFF_GA:IN_WIDTH]], axis=1)

    uq = w_uq[l].reshape(Q_LORA, N_HEADS, QK_NOPE + QK_ROPE)
    uq_nope = uq[:, :, :QK_NOPE].reshape(Q_LORA, N_HEADS * QK_NOPE)
    uq_rope = uq[:, :, QK_NOPE:]
    wq = jnp.concatenate([
        uq_nope,
        _pad_lanes(uq_rope).reshape(Q_LORA, N_HEADS * LANES),
        _pad_lanes(_rot_cols(uq_rope)).reshape(Q_LORA, N_HEADS * LANES)], axis=1)

    eye = jnp.eye(N_HEADS, dtype=F32)
    uk = w_uk[l].reshape(KV_LORA, N_HEADS, QK_NOPE)
    bduk = jnp.einsum("chd,hg->hdgc", uk, eye).reshape(N_HEADS * QK_NOPE, N_HEADS * KV_LORA)
    uv = w_uv[l].reshape(KV_LORA, N_HEADS, V_DIM)
    bduv = jnp.einsum("chd,hg->hcgd", uv, eye).reshape(N_HEADS * KV_LORA, N_HEADS * V_DIM)

    return dict(
        w_pack=pack(wi).astype(BF16), b_pack=pack(bi),
        q_norm=q_norm[l][None, :], kv_norm=kv_norm[l][None, :],
        wq=wq.astype(BF16), bduk=bduk.astype(BF16), bduv=bduv.astype(BF16),
        w_oa=w_oa[l].astype(BF16), w_ob=w_ob[l].astype(BF16), w_o=w_o[l].astype(BF16),
        conv_w=jnp.pad(conv_w[l], ((0, 8 - CONV_K), (0, 0))), conv_b=conv_b[l][None, :],
        ln1_g=ln1_g[l][None, :], ln1_b=ln1_b[l][None, :],
        ln2_g=ln2_g[l][None, :], ln2_b=ln2_b[l][None, :],
        w_router=_pad_lanes(w_router[l]),
        b_router=jnp.pad(b_router[l], (0, LANES - N_EXPERTS), constant_values=NEG_BIG)[None, :],
        w_gu=w_gu[l].astype(BF16), b_gu=b_gu[l], w_down=w_down[l].astype(BF16), b_down=b_down[l],
    )


def _rope_tables(n):
    n_rows = n // GRID_W
    row = jnp.broadcast_to(jnp.arange(n_rows, dtype=F32)[:, None], (n_rows, GRID_W)).reshape(-1)
    col = jnp.broadcast_to(jnp.arange(GRID_W, dtype=F32)[None, :], (n_rows, GRID_W)).reshape(-1)
    half = QK_ROPE // 2
    freqs = ROPE_THETA ** (-jnp.arange(0, half, 2, dtype=F32) / half)
    ang_r = row[:, None] * freqs
    ang_c = col[:, None] * freqs
    ang = jnp.concatenate([ang_r, ang_r, ang_c, ang_c], axis=-1)
    return _pad_lanes(jnp.cos(ang)), _pad_lanes(jnp.sin(ang))


def kernel(x, c, ctx, c_ctx, w_ada, b_ada, w_in, b_in, q_norm, kv_norm, w_uq, w_uk, w_uv, w_oa,
           conv_w, conv_b, w_ob, w_o, ln1_g, ln1_b, ln2_g, ln2_b, w_router, b_router,
           w_gu, b_gu, w_down, b_down):
    b, n, d = x.shape
    nc = ctx.shape[1]
    assert ROW_TILE == BLOCK_TOKENS and BLOCK_TOKENS % nc == 0 and (b * nc) % BLOCK_TOKENS == 0
    cvec = jnp.concatenate([c, c_ctx[None, :], jnp.zeros((8 - b - 1, d), F32)], axis=0)
    mod = _modulation(cvec, w_ada, b_ada)

    cos_x, sin_x = _rope_tables(n)
    cos_c = _pad_lanes(jnp.ones((nc, QK_ROPE), F32))
    sin_c = jnp.zeros((nc, LANES), F32)

    for l in range(DEPTH):
        update_ctx = l < DEPTH - 1
        lw = _pack_layer(l, w_in, b_in, q_norm, kv_norm, w_uq, w_uk, w_uv, w_oa, conv_w, conv_b,
                         w_ob, w_o, ln1_g, ln1_b, ln2_g, ln2_b, w_router, b_router,
                         w_gu, b_gu, w_down, b_down)
        sh1, sc1, g1, sh2, sc2, g2 = [m[:, None, :] for m in jnp.split(mod[l], 6, axis=-1)]
        xm = lambda m: m[:b]
        cm = lambda m: jnp.broadcast_to(m[b:b + 1], (b, 1, d))

        q_x, kt_x, v_x, zc_x, bg_x, gs_x = _in_projection(
            x, 1.0 + xm(sc1), xm(sh1), cos_x, sin_x, lw, ROW_TILE)
        q_c, kt_c, v_c, zc_c, bg_c, gs_c = _in_projection(
            ctx, 1.0 + cm(sc1), cm(sh1), cos_c, sin_c, lw, nc)
        kt_all = jnp.concatenate([kt_c, kt_x], axis=2)
        v_all = jnp.concatenate([v_c, v_x], axis=1)
        o_x = _latent_attention(q_x, kt_all, v_all, Q_TILE, KV_CHUNK)
        x1, hx, ridx, ridxt = _merge_ln_route(
            o_x, zc_x, bg_x, gs_x, x, xm(g1), 1.0 + xm(sc2), xm(sh2), lw, ROW_TILE)
        hx_all = hx.reshape(b * n, XROW_W)
        ridx_all = ridx.reshape(b * n, LANES)
        if update_ctx:
            o_c = _latent_attention(q_c, kt_c, v_c, nc, nc)
            c1, hxc, ridx_c, ridxt_c = _merge_ln_route(
                o_c, zc_c, bg_c, gs_c, ctx, cm(g1), 1.0 + cm(sc2), cm(sh2), lw, nc)
            per_block = BLOCK_TOKENS // nc
            ridxt_c = ridxt_c.reshape(b // per_block, per_block, 8, nc).transpose(0, 2, 1, 3)
            hx_all = jnp.concatenate([hx_all, hxc.reshape(b * nc, XROW_W)], axis=0)
            ridx_all = jnp.concatenate([ridx_all, ridx_c.reshape(b * nc, LANES)], axis=0)
            ridxt = jnp.concatenate([ridxt, ridxt_c.reshape(b // per_block, 8, BLOCK_TOKENS)], axis=0)

        tabs = _route_tables(ridx_all)
        ntiles = _num_expert_tiles(ridx_all.shape[0])
        xs = _dispatch(tabs, ridxt, hx_all, ntiles * EXPERT_TILE)
        ys = _expert_ffn(tabs, xs, lw["w_gu"], lw["b_gu"], lw["w_down"], lw["b_down"], ntiles)
        x = _combine_ln(tabs, ys, ridx_all, x1.reshape(b * n, d), xm(g2), n,
                        lw["ln2_g"], lw["ln2_b"], 0).reshape(b, n, d)
        if update_ctx:
            ctx = _combine_ln(tabs, ys, ridx_all, c1.reshape(b * nc, d), cm(g2), nc,
                              lw["ln2_g"], lw["ln2_b"], (b * n) // BLOCK_TOKENS).reshape(b, nc, d)
    return x
```

```python
import functools

import jax
import jax.numpy as jnp
from jax import lax
from jax.experimental import pallas as pl
from jax.experimental.pallas import tpu as pltpu

F32 = jnp.float32
BF16 = jnp.bfloat16
I32 = jnp.int32

D_MODEL = 1024
DEPTH = 2
GRID_W = 64
N_HEADS = 8
QK_NOPE = 64
QK_ROPE = 32
V_DIM = 64
Q_LORA = 256
KV_LORA = 128
ROPE_THETA = 10000.0
ATTN_SCALE = (QK_NOPE + QK_ROPE) ** -0.5
CONV_WIDTH = 512
CONV_K = 3
N_EXPERTS = 32
TOP_K = 4
D_FF = D_MODEL
SWIGLU_LIMIT = 7.0
SWIGLU_ALPHA = 1.702
NORM_EPS = 1e-6
DEEPNORM_ALPHA = (2 * DEPTH) ** 0.25

OFF_QA = 0
OFF_KV = OFF_QA + Q_LORA
OFF_KR = OFF_KV + KV_LORA
OFF_CX = OFF_KR + QK_ROPE
OFF_CB = OFF_CX + CONV_WIDTH
OFF_CC = OFF_CB + CONV_WIDTH
OFF_GA = OFF_CC + CONV_WIDTH
OFF_GB = OFF_GA + D_MODEL
IN_WIDTH = OFF_GB + D_MODEL

LANES = 128
QK_WIDTH = 2 * LANES
LOG2E = 1.4426950408889634
NEG_BIG = -1e30
VMEM_LIMIT = 56 * 1024 * 1024

PA_W = Q_LORA + KV_LORA + 2 * LANES
PC_OFF = PA_W
PC_W = 3 * CONV_WIDTH
PG_OFF = PC_OFF + PC_W
PG_W = 2 * D_MODEL
PACK_W = PG_OFF + PG_W

ROW_TILE = 512
Q_TILE = 2048
KV_CHUNK = 768
EXPERT_TILE = 512
BLOCK_TOKENS = 512
XROW_W = D_MODEL + LANES
STRIP_ROWS = 16
LOCAL_ROWS = TOP_K * BLOCK_TOKENS + N_EXPERTS * STRIP_ROWS
STRIP_BITS = 8


def _dot(a, b):
    return jnp.dot(a, b, preferred_element_type=F32)


def _split_bf16(a):
    hi = a.astype(BF16)
    lo = (a - hi.astype(F32)).astype(BF16)
    return hi, lo


def _dot3(a, b):
    ah, al = _split_bf16(a)
    bh, bl = _split_bf16(b)
    return _dot(ah, bh) + (_dot(ah, bl) + _dot(al, bh))


def _rms_norm(x, g):
    return x * lax.rsqrt(jnp.mean(jnp.square(x), axis=-1, keepdims=True) + NORM_EPS) * g


def _layer_norm(x, g, b):
    mu = jnp.mean(x, axis=-1, keepdims=True)
    xc = x - mu
    var = jnp.mean(jnp.square(xc), axis=-1, keepdims=True)
    return xc * lax.rsqrt(var + NORM_EPS) * g + b


def _const_spec(shape):
    nd = len(shape)
    return pl.BlockSpec(shape, lambda *_: (0,) * nd, pipeline_mode=pl.Buffered(1))


def _mod_kernel(c_ref, w_ref, b_ref, o_ref):
    c = c_ref[...]
    s = c * jax.nn.sigmoid(c)
    o_ref[0] = _dot3(s, w_ref[0]) + b_ref[0]


def _modulation(cvec, w_ada, b_ada):
    depth, d, n = w_ada.shape
    tn = 1536
    return pl.pallas_call(
        _mod_kernel,
        out_shape=jax.ShapeDtypeStruct((depth, 8, n), F32),
        grid=(depth, n // tn),
        in_specs=[
            pl.BlockSpec((8, d), lambda l, j: (0, 0)),
            pl.BlockSpec((1, d, tn), lambda l, j: (l, 0, j)),
            pl.BlockSpec((1, 1, tn), lambda l, j: (l, 0, j)),
        ],
        out_specs=pl.BlockSpec((1, 8, tn), lambda l, j: (l, 0, j)),
        compiler_params=pltpu.CompilerParams(
            dimension_semantics=("arbitrary", "arbitrary"), vmem_limit_bytes=VMEM_LIMIT),
        name="adaln_mod",
    )(cvec, w_ada, b_ada.reshape(depth, 1, n))


def _inproj_kernel(x_ref, scp_ref, sh_ref, cos_ref, sin_ref, w_ref, b_ref, qn_ref, kvn_ref,
                   wq_ref, bduk_ref,
                   q_ref, kt_ref, v_ref, zc_ref, bg_ref, gs_ref):
    h = (x_ref[0] * scp_ref[0] + sh_ref[0]).astype(BF16)
    cos = cos_ref[...]
    sin = sin_ref[...]

    pa = _dot(h, w_ref[:, 0:PA_W]) + b_ref[:, 0:PA_W]
    qn = _rms_norm(pa[:, 0:Q_LORA], qn_ref[...]).astype(BF16)
    ckv = _rms_norm(pa[:, Q_LORA:Q_LORA + KV_LORA], kvn_ref[...])
    kr = pa[:, Q_LORA + KV_LORA:Q_LORA + KV_LORA + LANES]
    krr = pa[:, Q_LORA + KV_LORA + LANES:PA_W]
    k_rope = kr * cos + krr * sin
    k_row = jnp.concatenate([ckv, k_rope], axis=1)
    kt_ref[0] = k_row.T.astype(BF16)
    ones_col = jnp.where(lax.broadcasted_iota(I32, ckv.shape, 1) == 0, 1.0, 0.0)
    v_ref[0] = jnp.concatenate([ckv, ones_col], axis=1).astype(BF16)

    nope_w = N_HEADS * QK_NOPE
    blk_w = N_HEADS * LANES
    q_all = _dot(qn, wq_ref[...])
    q_lat = _dot(q_all[:, 0:nope_w].astype(BF16), bduk_ref[...])
    qscale = ATTN_SCALE * LOG2E
    for hd in range(N_HEADS):
        lo = hd * LANES
        qr = q_all[:, nope_w + lo:nope_w + lo + LANES]
        qrr = q_all[:, nope_w + blk_w + lo:nope_w + blk_w + lo + LANES]
        q_rope = qr * cos + qrr * sin
        q_row = jnp.concatenate([q_lat[:, lo:lo + LANES], q_rope], axis=1) * qscale
        q_ref[0, hd] = q_row.astype(BF16)

    pc = _dot(h, w_ref[:, PC_OFF:PC_OFF + PC_W]) + b_ref[:, PC_OFF:PC_OFF + PC_W]
    zc_ref[0] = pc[:, 2 * CONV_WIDTH:3 * CONV_WIDTH] * pc[:, 0:CONV_WIDTH]
    bg_ref[0] = pc[:, CONV_WIDTH:2 * CONV_WIDTH]

    pg = _dot(h, w_ref[:, PG_OFF:PG_OFF + PG_W]) + b_ref[:, PG_OFF:PG_OFF + PG_W]
    gs_ref[0] = jax.nn.sigmoid(pg).astype(BF16)


def _in_projection(x, scp, sh, cos, sin, lw, tm):
    b, s, d = x.shape
    row = lambda bi, i: (bi, i, 0)
    per_b = lambda bi, i: (bi, 0, 0)
    return pl.pallas_call(
        _inproj_kernel,
        out_shape=(
            jax.ShapeDtypeStruct((b, N_HEADS, s, QK_WIDTH), BF16),
            jax.ShapeDtypeStruct((b, QK_WIDTH, s), BF16),
            jax.ShapeDtypeStruct((b, s, QK_WIDTH), BF16),
            jax.ShapeDtypeStruct((b, s, CONV_WIDTH), F32),
            jax.ShapeDtypeStruct((b, s, CONV_WIDTH), F32),
            jax.ShapeDtypeStruct((b, s, 2 * D_MODEL), BF16),
        ),
        grid=(b, s // tm),
        in_specs=[
            pl.BlockSpec((1, tm, d), row),
            pl.BlockSpec((1, 1, d), per_b),
            pl.BlockSpec((1, 1, d), per_b),
            pl.BlockSpec((tm, LANES), lambda bi, i: (i, 0)),
            pl.BlockSpec((tm, LANES), lambda bi, i: (i, 0)),
            _const_spec((d, PACK_W)),
            _const_spec((1, PACK_W)),
            _const_spec((1, Q_LORA)),
            _const_spec((1, KV_LORA)),
            _const_spec(lw["wq"].shape),
            _const_spec(lw["bduk"].shape),
        ],
        out_specs=(
            pl.BlockSpec((1, N_HEADS, tm, QK_WIDTH), lambda bi, i: (bi, 0, i, 0)),
            pl.BlockSpec((1, QK_WIDTH, tm), lambda bi, i: (bi, 0, i)),
            pl.BlockSpec((1, tm, QK_WIDTH), row),
            pl.BlockSpec((1, tm, CONV_WIDTH), row),
            pl.BlockSpec((1, tm, CONV_WIDTH), row),
            pl.BlockSpec((1, tm, 2 * D_MODEL), row),
        ),
        compiler_params=pltpu.CompilerParams(
            dimension_semantics=("arbitrary", "arbitrary"), vmem_limit_bytes=VMEM_LIMIT),
        name="in_projection",
    )(x, scp, sh, cos, sin, lw["w_pack"], lw["b_pack"], lw["q_norm"], lw["kv_norm"],
      lw["wq"], lw["bduk"])


def _attn_kernel(q_ref, kt_ref, v_ref, o_ref, s0_ref, s1_ref, m_ref, acc_ref, *, tk, nchunks):
    q = q_ref[0, 0]

    def scores(c, s_ref):
        off = pl.multiple_of(c * tk, tk)
        s_ref[...] = _dot(q, kt_ref[0, :, pl.ds(off, tk)])

    def absorb(c, s_ref):
        off = pl.multiple_of(c * tk, tk)
        tiles = [s_ref[:, j * LANES:(j + 1) * LANES] for j in range(tk // LANES)]
        m_tile = functools.reduce(jnp.maximum, tiles)
        m_old = m_ref[...]
        m_new = jnp.maximum(m_old, jnp.broadcast_to(jnp.max(m_tile, axis=-1, keepdims=True),
                                                    m_old.shape))
        m_ref[...] = m_new
        p = jnp.concatenate([jnp.exp2(t - m_new).astype(BF16) for t in tiles], axis=1)
        alpha = jnp.exp2(m_old - m_new)
        pv = _dot(p, v_ref[0, pl.ds(off, tk), :])
        acc_ref[...] = jnp.concatenate([alpha, alpha], axis=1) * acc_ref[...] + pv

    m_ref[...] = jnp.full(m_ref.shape, NEG_BIG, F32)
    acc_ref[...] = jnp.zeros(acc_ref.shape, F32)
    scores(0, s0_ref)
    npairs = (nchunks - 1) // 2

    def pair(i, carry):
        c = 2 * i
        scores(c + 1, s1_ref)
        absorb(c, s0_ref)
        scores(c + 2, s0_ref)
        absorb(c + 1, s1_ref)
        return carry

    lax.fori_loop(0, npairs, pair, 0)
    if nchunks % 2 == 0:
        scores(nchunks - 1, s1_ref)
        absorb(nchunks - 2, s0_ref)
        absorb(nchunks - 1, s1_ref)
    else:
        absorb(nchunks - 1, s0_ref)
    acc = acc_ref[...]
    o_ref[0] = (acc[:, 0:KV_LORA] / acc[:, KV_LORA:KV_LORA + 1]).astype(BF16)


def _latent_attention(q, kt, v, tq, tk):
    b, nh, s, _ = q.shape
    skv = kt.shape[2]
    return pl.pallas_call(
        functools.partial(_attn_kernel, tk=tk, nchunks=skv // tk),
        out_shape=jax.ShapeDtypeStruct((b, s, nh * KV_LORA), BF16),
        grid=(b, nh, s // tq),
        in_specs=[
            pl.BlockSpec((1, 1, tq, QK_WIDTH), lambda bi, hi, i: (bi, hi, i, 0)),
            pl.BlockSpec((1, QK_WIDTH, skv), lambda bi, hi, i: (bi, 0, 0)),
            pl.BlockSpec((1, skv, QK_WIDTH), lambda bi, hi, i: (bi, 0, 0)),
        ],
        out_specs=pl.BlockSpec((1, tq, KV_LORA), lambda bi, hi, i: (bi, i, hi)),
        scratch_shapes=[
            pltpu.VMEM((tq, tk), F32),
            pltpu.VMEM((tq, tk), F32),
            pltpu.VMEM((tq, LANES), F32),
            pltpu.VMEM((tq, QK_WIDTH), F32),
        ],
        compiler_params=pltpu.CompilerParams(
            dimension_semantics=("arbitrary", "arbitrary", "arbitrary"),
            vmem_limit_bytes=VMEM_LIMIT),
        name="latent_attention",
    )(q, kt, v)


def _merge_kernel(o_ref, zc_ref, zp_ref, zn_ref, bg_ref, gs_ref, x_ref, g1_ref, scp2_ref, sh2_ref,
                  bduv_ref, woa_ref, wob_ref, wo_ref, cw_ref, cb_ref, lng_ref, lnb_ref,
                  wr_ref, br_ref,
                  x1_ref, hx_ref, ridx_ref, ridxt_ref, *, ntiles):
    i = pl.program_id(1)
    tm = x_ref.shape[1]

    attn = _dot(o_ref[0], bduv_ref[...]).astype(BF16)
    y_a = _dot(attn, woa_ref[...])

    zc = zc_ref[0]
    rows = lax.broadcasted_iota(I32, zc.shape, 0)
    prev_row = jnp.where(i > 0, zp_ref[0][7:8, :], 0.0)
    next_row = jnp.where(i < ntiles - 1, zn_ref[0][0:1, :], 0.0)
    z_prev = jnp.where(rows == 0, prev_row, pltpu.roll(zc, 1, axis=0))
    z_next = jnp.where(rows == tm - 1, next_row, pltpu.roll(zc, tm - 1, axis=0))
    cw = cw_ref[...]
    conv = bg_ref[0] * (z_prev * cw[0:1] + zc * cw[1:2] + z_next * cw[2:3] + cb_ref[...])
    y_b = _dot(conv.astype(BF16), wob_ref[...])

    gs = gs_ref[0].astype(F32)
    merged = gs[:, 0:D_MODEL] * y_a + gs[:, D_MODEL:2 * D_MODEL] * y_b
    y = _dot(merged.astype(BF16), wo_ref[...])

    x1 = _layer_norm(DEEPNORM_ALPHA * x_ref[0] + g1_ref[0] * y, lng_ref[...], lnb_ref[...])
    x1_ref[0] = x1
    h2 = x1 * scp2_ref[0] + sh2_ref[0]

    logits = _dot3(h2, wr_ref[...]) + br_ref[...]
    lanes = lax.broadcasted_iota(I32, logits.shape, 1).astype(F32)
    sels, exps = [], []
    top = None
    denom = None
    for k in range(TOP_K):
        mx = jnp.max(logits, axis=-1, keepdims=True)
        sel = jnp.min(jnp.where(logits == mx, lanes, float(LANES)), axis=-1, keepdims=True)
        if k == 0:
            top = mx
            e = jnp.ones_like(mx)
            denom = e
        else:
            e = jnp.exp(mx - top)
            denom = denom + e
        sels.append(sel)
        exps.append(e)
        logits = jnp.where(lanes == sel, NEG_BIG * 2.0, logits)

    idx_out = jnp.zeros(logits.shape, F32)
    meta = jnp.zeros(logits.shape, F32)
    for k in range(TOP_K):
        g = exps[k] / denom
        hi = g.astype(BF16).astype(F32)
        r1 = g - hi
        mid = r1.astype(BF16).astype(F32)
        lo = (r1 - mid).astype(BF16).astype(F32)
        idx_out = jnp.where(lanes == k, sels[k], idx_out)
        meta = jnp.where(lanes == TOP_K + k, hi, meta)
        meta = jnp.where(lanes == 2 * TOP_K + k, mid, meta)
        meta = jnp.where(lanes == 3 * TOP_K + k, lo, meta)
    meta = meta + idx_out
    ridx_ref[0] = idx_out.astype(I32)
    ridxt_ref[0] = idx_out.T[0:8, :].astype(I32)
    hx_ref[0] = jnp.concatenate([h2.astype(BF16), meta.astype(BF16)], axis=1)


def _merge_ln_route(o, zc, bg, gs, x, g1, scp2, sh2, lw, tm):
    b, s, d = x.shape
    ntiles = s // tm
    h8 = tm // 8
    row = lambda bi, i: (bi, i, 0)
    per_b = lambda bi, i: (bi, 0, 0)
    consts = [lw["bduv"], lw["w_oa"], lw["w_ob"], lw["w_o"], lw["conv_w"], lw["conv_b"],
              lw["ln1_g"], lw["ln1_b"], lw["w_router"], lw["b_router"]]
    return pl.pallas_call(
        functools.partial(_merge_kernel, ntiles=ntiles),
        out_shape=(
            jax.ShapeDtypeStruct((b, s, d), F32),
            jax.ShapeDtypeStruct((b, s, XROW_W), BF16),
            jax.ShapeDtypeStruct((b, s, LANES), I32),
            jax.ShapeDtypeStruct((b * ntiles, 8, tm), I32),
        ),
        grid=(b, ntiles),
        in_specs=[
            pl.BlockSpec((1, tm, N_HEADS * KV_LORA), row),
            pl.BlockSpec((1, tm, CONV_WIDTH), row),
            pl.BlockSpec((1, 8, CONV_WIDTH), lambda bi, i: (bi, jnp.maximum(i * h8 - 1, 0), 0)),
            pl.BlockSpec((1, 8, CONV_WIDTH),
                         lambda bi, i: (bi, jnp.minimum((i + 1) * h8, s // 8 - 1), 0)),
            pl.BlockSpec((1, tm, CONV_WIDTH), row),
            pl.BlockSpec((1, tm, 2 * D_MODEL), row),
            pl.BlockSpec((1, tm, d), row),
            pl.BlockSpec((1, 1, d), per_b),
            pl.BlockSpec((1, 1, d), per_b),
            pl.BlockSpec((1, 1, d), per_b),
        ] + [_const_spec(w.shape) for w in consts],
        out_specs=(
            pl.BlockSpec((1, tm, d), row),
            pl.BlockSpec((1, tm, XROW_W), row),
            pl.BlockSpec((1, tm, LANES), row),
            pl.BlockSpec((1, 8, tm), lambda bi, i: (bi * ntiles + i, 0, 0)),
        ),
        compiler_params=pltpu.CompilerParams(
            dimension_semantics=("arbitrary", "arbitrary"), vmem_limit_bytes=VMEM_LIMIT),
        name="merge_ln_route",
    )(o, zc, zc, zc, bg, gs, x, g1, scp2, sh2, *consts)


def _strip_copies(units, src_ref, src_row, dst_ref, dst_row, sem):
    for bit in reversed(range(STRIP_BITS)):
        done = ((units >> (bit + 1)) << (bit + 1)) * STRIP_ROWS

        @pl.when(((units >> bit) & 1) == 1)
        def _(bit=bit, done=done):
            s = pl.multiple_of(src_row + done, STRIP_ROWS)
            d = pl.multiple_of(dst_row + done, STRIP_ROWS)
            size = STRIP_ROWS << bit
            pltpu.make_async_copy(src_ref.at[pl.ds(s, size)], dst_ref.at[pl.ds(d, size)], sem).start()


def _dispatch_kernel(goff_ref, nu_ref, lst_ref, tot_ref, ridxt_ref, lcol_ref, hx_ref, zeros_hbm,
                     xs_hbm, buf, sem, *, dump_row):
    del zeros_hbm
    b = pl.program_id(0)
    tb = hx_ref.shape[0]
    ridxt = ridxt_ref[0]
    e_iota = lax.broadcasted_iota(I32, (N_EXPERTS, tb), 0)
    hits = [ridxt[k:k + 1, :] == e_iota for k in range(TOP_K)]
    mt = sum(jnp.where(h, 1.0, 0.0) for h in hits)
    earlier = lax.broadcasted_iota(I32, (tb, tb), 0) < lax.broadcasted_iota(I32, (tb, tb), 1)
    rank = _dot(mt.astype(BF16), jnp.where(earlier, 1.0, 0.0).astype(BF16))
    base = lcol_ref[0] + rank
    slot = [jnp.sum(jnp.where(h, base, 0.0), axis=0, keepdims=True) for h in hits]

    j = lax.broadcasted_iota(I32, (LOCAL_ROWS, tb), 0).astype(F32)
    sel = jnp.where(j == slot[0], 1.0,
                    jnp.where(j == slot[1], 1.0,
                              jnp.where(j == slot[2], 1.0, jnp.where(j == slot[3], 1.0, 0.0))))
    buf[...] = _dot(sel.astype(BF16), hx_ref[...]).astype(BF16)

    for e in range(N_EXPERTS):
        t = b * N_EXPERTS + e
        _strip_copies(nu_ref[t], buf, lst_ref[t], xs_hbm, goff_ref[t], sem)
    tot = tot_ref[b]
    _strip_copies(LOCAL_ROWS // STRIP_ROWS - tot, buf, tot * STRIP_ROWS, xs_hbm,
                  dump_row + tot * STRIP_ROWS, sem)
    pltpu.make_async_copy(buf, xs_hbm.at[pl.ds(0, LOCAL_ROWS)], sem).wait()


def _dispatch(tabs, ridxt, hx, nslots):
    t, w = hx.shape
    tb = BLOCK_TOKENS
    nb = t // tb
    grid_spec = pltpu.PrefetchScalarGridSpec(
        num_scalar_prefetch=4,
        grid=(nb,),
        in_specs=[
            pl.BlockSpec((1, 8, tb), lambda i, *_: (i, 0, 0)),
            pl.BlockSpec((1, N_EXPERTS, 1), lambda i, *_: (i, 0, 0)),
            pl.BlockSpec((tb, w), lambda i, *_: (i, 0)),
            pl.BlockSpec(memory_space=pl.ANY),
        ],
        out_specs=pl.BlockSpec(memory_space=pl.ANY),
        scratch_shapes=[pltpu.VMEM((LOCAL_ROWS, w), BF16), pltpu.SemaphoreType.DMA(())],
    )
    out_shape = jax.ShapeDtypeStruct((nslots + LOCAL_ROWS, w), BF16)
    return pl.pallas_call(
        functools.partial(_dispatch_kernel, dump_row=nslots),
        out_shape=out_shape,
        grid_spec=grid_spec,
        input_output_aliases={7: 0},
        compiler_params=pltpu.CompilerParams(
            dimension_semantics=("arbitrary",), vmem_limit_bytes=VMEM_LIMIT),
        name="moe_dispatch",
    )(tabs["goff"], tabs["nu"], tabs["lst"], tabs["tot"], ridxt, tabs["lcol"], hx,
      jnp.zeros(out_shape.shape, out_shape.dtype))


def _expert_kernel(te_ref, nv_ref, x_ref, wgu_ref, bgu_ref, wd_ref, bd_ref, y_ref):
    j = pl.program_id(0)
    nv = nv_ref[j]

    @pl.when(nv > 0)
    def _():
        meta = x_ref[:, D_MODEL:XROW_W].astype(F32)
        lanes = lax.broadcasted_iota(I32, meta.shape, 1)
        gates = (pltpu.roll(meta, LANES - TOP_K, axis=1) + pltpu.roll(meta, LANES - 2 * TOP_K, axis=1)
                 + pltpu.roll(meta, LANES - 3 * TOP_K, axis=1))
        mine = (lanes < TOP_K) & (meta == te_ref[j].astype(F32))
        g = jnp.sum(jnp.where(mine, gates, 0.0), axis=-1, keepdims=True)

        gu = _dot(x_ref[:, 0:D_MODEL], wgu_ref[0]) + bgu_ref[0]
        gate = jnp.minimum(gu[:, 0:D_FF], SWIGLU_LIMIT)
        up = jnp.clip(gu[:, D_FF:2 * D_FF], -SWIGLU_LIMIT, SWIGLU_LIMIT)
        act = (up + 1.0) * gate * jax.nn.sigmoid(SWIGLU_ALPHA * gate)
        y_ref[...] = (g * (_dot(act.astype(BF16), wd_ref[0]) + bd_ref[0])).astype(BF16)

    @pl.when(nv <= 0)
    def _():
        y_ref[...] = jnp.zeros_like(y_ref)


def _expert_ffn(tabs, xs, wgu, bgu, wd, bd, ntiles):
    tme = EXPERT_TILE
    w = xs.shape[1]
    ne, d, f2 = wgu.shape
    f = wd.shape[1]
    grid_spec = pltpu.PrefetchScalarGridSpec(
        num_scalar_prefetch=2,
        grid=(ntiles,),
        in_specs=[
            pl.BlockSpec((tme, w), lambda j, te, nv: (j, 0)),
            pl.BlockSpec((1, d, f2), lambda j, te, nv: (te[j], 0, 0)),
            pl.BlockSpec((1, 1, f2), lambda j, te, nv: (te[j], 0, 0)),
            pl.BlockSpec((1, f, d), lambda j, te, nv: (te[j], 0, 0)),
            pl.BlockSpec((1, 1, d), lambda j, te, nv: (te[j], 0, 0)),
        ],
        out_specs=pl.BlockSpec((tme, d), lambda j, te, nv: (j, 0)),
    )
    return pl.pallas_call(
        _expert_kernel,
        out_shape=jax.ShapeDtypeStruct((ntiles * tme, d), BF16),
        grid_spec=grid_spec,
        compiler_params=pltpu.CompilerParams(
            dimension_semantics=("arbitrary",), vmem_limit_bytes=VMEM_LIMIT),
        name="expert_ffn",
    )(tabs["te"], tabs["nv"], xs, wgu, bgu.reshape(ne, 1, f2), wd, bd.reshape(ne, 1, d))


def _combine_kernel(goff_ref, nu_ref, lst_ref, tot_ref, ys_hbm, ridx_ref, lrow_ref, x1_ref, g2_ref,
                    lng_ref, lnb_ref, o_ref, ybuf, sem, *, block0):
    b = pl.program_id(0) + block0
    tb = x1_ref.shape[0]
    for e in range(N_EXPERTS):
        t = b * N_EXPERTS + e
        _strip_copies(nu_ref[t], ys_hbm, goff_ref[t], ybuf, lst_ref[t], sem)
    tot = tot_ref[b]
    _strip_copies(LOCAL_ROWS // STRIP_ROWS - tot, ys_hbm, 0, ybuf, tot * STRIP_ROWS, sem)

    ridx = ridx_ref[...]
    lanes = lax.broadcasted_iota(I32, ridx.shape, 1)
    hits = [ridx[:, k:k + 1] == lanes for k in range(TOP_K)]
    m = sum(jnp.where(h, 1.0, 0.0) for h in hits)
    earlier = lax.broadcasted_iota(I32, (tb, tb), 1) < lax.broadcasted_iota(I32, (tb, tb), 0)
    rank = _dot(jnp.where(earlier, 1.0, 0.0).astype(BF16), m.astype(BF16))
    base = lrow_ref[0] + rank
    slot = [jnp.sum(jnp.where(h, base, 0.0), axis=-1, keepdims=True) for h in hits]
    j = lax.broadcasted_iota(I32, (tb, LOCAL_ROWS), 1).astype(F32)
    sel = jnp.where(j == slot[0], 1.0,
                    jnp.where(j == slot[1], 1.0,
                              jnp.where(j == slot[2], 1.0, jnp.where(j == slot[3], 1.0, 0.0))))
    sel = sel.astype(BF16)

    pltpu.make_async_copy(ys_hbm.at[pl.ds(0, LOCAL_ROWS)], ybuf, sem).wait()
    f = _dot(sel, ybuf[...])
    r = DEEPNORM_ALPHA * x1_ref[...] + g2_ref[0] * f
    o_ref[...] = _layer_norm(r, lng_ref[...], lnb_ref[...])


def _combine_ln(tabs, ys, ridx_all, x1, g2, rows_per_mod, ln_g, ln_b, block0):
    t, d = x1.shape
    tb = BLOCK_TOKENS
    grid_spec = pltpu.PrefetchScalarGridSpec(
        num_scalar_prefetch=4,
        grid=(t // tb,),
        in_specs=[
            pl.BlockSpec(memory_space=pl.ANY),
            pl.BlockSpec((tb, LANES), lambda i, *_: (i + block0, 0)),
            pl.BlockSpec((1, 1, LANES), lambda i, *_: (i + block0, 0, 0)),
            pl.BlockSpec((tb, d), lambda i, *_: (i, 0)),
            pl.BlockSpec((1, 1, d), lambda i, *_: ((i * tb) // rows_per_mod, 0, 0)),
            pl.BlockSpec((1, d), lambda i, *_: (0, 0)),
            pl.BlockSpec((1, d), lambda i, *_: (0, 0)),
        ],
        out_specs=pl.BlockSpec((tb, d), lambda i, *_: (i, 0)),
        scratch_shapes=[pltpu.VMEM((LOCAL_ROWS, d), BF16), pltpu.SemaphoreType.DMA(())],
    )
    return pl.pallas_call(
        functools.partial(_combine_kernel, block0=block0),
        out_shape=jax.ShapeDtypeStruct((t, d), F32),
        grid_spec=grid_spec,
        compiler_params=pltpu.CompilerParams(
            dimension_semantics=("arbitrary",), vmem_limit_bytes=VMEM_LIMIT),
        name="moe_combine_ln",
    )(tabs["goff"], tabs["nu"], tabs["lst"], tabs["tot"], ys, ridx_all, tabs["lrow"], x1, g2,
      ln_g, ln_b)


def _num_expert_tiles(t):
    nb = t // BLOCK_TOKENS
    return -(-(TOP_K * t + nb * N_EXPERTS * (STRIP_ROWS - 1)) // EXPERT_TILE) + N_EXPERTS


def _route_tables(ridx_all):
    t = ridx_all.shape[0]
    tb, tme = BLOCK_TOKENS, EXPERT_TILE
    nb = t // tb
    experts = jnp.arange(N_EXPERTS, dtype=I32)
    onehot = jnp.sum((ridx_all[:, :TOP_K, None] == experts[None, None, :]).astype(I32), axis=1)
    u = STRIP_ROWS
    nu = (jnp.sum(onehot.reshape(nb, tb, N_EXPERTS), axis=1) + u - 1) // u
    lstu = jnp.cumsum(nu, axis=1) - nu
    tot = jnp.sum(nu, axis=1)
    gu = jnp.sum(nu, axis=0)
    tu = tme // u
    tiles = (gu + tu - 1) // tu
    tile_end = jnp.cumsum(tiles)
    tile_first = tile_end - tiles
    goffu = (tile_first * tu)[None, :] + jnp.cumsum(nu, axis=0) - nu
    ntiles = _num_expert_tiles(t)
    jt = jnp.arange(ntiles, dtype=I32)
    te = jnp.minimum(jnp.sum((jt[:, None] >= tile_end[None, :]).astype(I32), axis=1), N_EXPERTS - 1)
    nv = jnp.clip(gu[te] * u - (jt - tile_first[te]) * tme, 0, tme)
    nv = jnp.where(jt < tile_end[-1], nv, 0)
    lst = (lstu * u).astype(I32)
    return dict(
        goff=(goffu * u).astype(I32).reshape(-1), nu=nu.astype(I32).reshape(-1), lst=lst.reshape(-1),
        tot=tot.astype(I32), te=te.astype(I32), nv=nv.astype(I32),
        lcol=lst.astype(F32)[:, :, None],
        lrow=_pad_lanes(lst.astype(F32))[:, None, :],
    )


def _rot_cols(w):
    q = QK_ROPE // 4
    w1, w2, w3, w4 = w[..., 0:q], w[..., q:2 * q], w[..., 2 * q:3 * q], w[..., 3 * q:4 * q]
    return jnp.concatenate([-w2, w1, -w4, w3], axis=-1)


def _pad_lanes(w, width=LANES):
    return jnp.pad(w, [(0, 0)] * (w.ndim - 1) + [(0, width - w.shape[-1])])


def _pack_layer(l, w_in, b_in, q_norm, kv_norm, w_uq, w_uk, w_uv, w_oa, conv_w, conv_b, w_ob, w_o,
                ln1_g, ln1_b, ln2_g, ln2_b, w_router, b_router, w_gu, b_gu, w_down, b_down):
    wi, bi = w_in[l], b_in[l][None, :]

    def pack(m):
        kr = m[:, OFF_KR:OFF_CX]
        return jnp.concatenate([
            m[:, OFF_QA:OFF_KR], _pad_lanes(kr), _pad_lanes(_rot_cols(kr)),
            m[:, OFF_CX:OFF_GA], m[:, OFF_GA:IN_WIDTH]], axis=1)

    uq = w_uq[l].reshape(Q_LORA, N_HEADS, QK_NOPE + QK_ROPE)
    uq_nope = uq[:, :, :QK_NOPE].reshape(Q_LORA, N_HEADS * QK_NOPE)
    uq_rope = uq[:, :, QK_NOPE:]
    wq = jnp.concatenate([
        uq_nope,
        _pad_lanes(uq_rope).reshape(Q_LORA, N_HEADS * LANES),
        _pad_lanes(_rot_cols(uq_rope)).reshape(Q_LORA, N_HEADS * LANES)], axis=1)

    eye = jnp.eye(N_HEADS, dtype=F32)
    uk = w_uk[l].reshape(KV_LORA, N_HEADS, QK_NOPE)
    bduk = jnp.einsum("chd,hg->hdgc", uk, eye).reshape(N_HEADS * QK_NOPE, N_HEADS * KV_LORA)
    uv = w_uv[l].reshape(KV_LORA, N_HEADS, V_DIM)
    bduv = jnp.einsum("chd,hg->hcgd", uv, eye).reshape(N_HEADS * KV_LORA, N_HEADS * V_DIM)

    return dict(
        w_pack=pack(wi).astype(BF16), b_pack=pack(bi),
        q_norm=q_norm[l][None, :], kv_norm=kv_norm[l][None, :],
        wq=wq.astype(BF16), bduk=bduk.astype(BF16), bduv=bduv.astype(BF16),
        w_oa=w_oa[l].astype(BF16), w_ob=w_ob[l].astype(BF16), w_o=w_o[l].astype(BF16),
        conv_w=jnp.pad(conv_w[l], ((0, 8 - CONV_K), (0, 0))), conv_b=conv_b[l][None, :],
        ln1_g=ln1_g[l][None, :], ln1_b=ln1_b[l][None, :],
        ln2_g=ln2_g[l][None, :], ln2_b=ln2_b[l][None, :],
        w_router=_pad_lanes(w_router[l]),
        b_router=jnp.pad(b_router[l], (0, LANES - N_EXPERTS), constant_values=NEG_BIG)[None, :],
        w_gu=w_gu[l].astype(BF16), b_gu=b_gu[l], w_down=w_down[l].astype(BF16), b_down=b_down[l],
    )


def _rope_tables(n):
    n_rows = n // GRID_W
    row = jnp.broadcast_to(jnp.arange(n_rows, dtype=F32)[:, None], (n_rows, GRID_W)).reshape(-1)
    col = jnp.broadcast_to(jnp.arange(GRID_W, dtype=F32)[None, :], (n_rows, GRID_W)).reshape(-1)
    half = QK_ROPE // 2
    freqs = ROPE_THETA ** (-jnp.arange(0, half, 2, dtype=F32) / half)
    ang_r = row[:, None] * freqs
    ang_c = col[:, None] * freqs
    ang = jnp.concatenate([ang_r, ang_r, ang_c, ang_c], axis=-1)
    return _pad_lanes(jnp.cos(ang)), _pad_lanes(jnp.sin(ang))


def kernel(x, c, ctx, c_ctx, w_ada, b_ada, w_in, b_in, q_norm, kv_norm, w_uq, w_uk, w_uv, w_oa,
           conv_w, conv_b, w_ob, w_o, ln1_g, ln1_b, ln2_g, ln2_b, w_router, b_router,
           w_gu, b_gu, w_down, b_down):
    b, n, d = x.shape
    nc = ctx.shape[1]
    assert ROW_TILE == BLOCK_TOKENS and BLOCK_TOKENS % nc == 0 and (b * nc) % BLOCK_TOKENS == 0
    cvec = jnp.concatenate([c, c_ctx[None, :], jnp.zeros((8 - b - 1, d), F32)], axis=0)
    mod = _modulation(cvec, w_ada, b_ada)

    cos_x, sin_x = _rope_tables(n)
    cos_c = _pad_lanes(jnp.ones((nc, QK_ROPE), F32))
    sin_c = jnp.zeros((nc, LANES), F32)

    for l in range(DEPTH):
        update_ctx = l < DEPTH - 1
        lw = _pack_layer(l, w_in, b_in, q_norm, kv_norm, w_uq, w_uk, w_uv, w_oa, conv_w, conv_b,
                         w_ob, w_o, ln1_g, ln1_b, ln2_g, ln2_b, w_router, b_router,
                         w_gu, b_gu, w_down, b_down)
        sh1, sc1, g1, sh2, sc2, g2 = [m[:, None, :] for m in jnp.split(mod[l], 6, axis=-1)]
        xm = lambda m: m[:b]
        cm = lambda m: jnp.broadcast_to(m[b:b + 1], (b, 1, d))

        q_x, kt_x, v_x, zc_x, bg_x, gs_x = _in_projection(
            x, 1.0 + xm(sc1), xm(sh1), cos_x, sin_x, lw, ROW_TILE)
        q_c, kt_c, v_c, zc_c, bg_c, gs_c = _in_projection(
            ctx, 1.0 + cm(sc1), cm(sh1), cos_c, sin_c, lw, nc)
        kt_all = jnp.concatenate([kt_c, kt_x], axis=2)
        v_all = jnp.concatenate([v_c, v_x], axis=1)
        o_x = _latent_attention(q_x, kt_all, v_all, Q_TILE, KV_CHUNK)
        x1, hx, ridx, ridxt = _merge_ln_route(
            o_x, zc_x, bg_x, gs_x, x, xm(g1), 1.0 + xm(sc2), xm(sh2), lw, ROW_TILE)
        hx_all = hx.reshape(b * n, XROW_W)
        ridx_all = ridx.reshape(b * n, LANES)
        if update_ctx:
            o_c = _latent_attention(q_c, kt_c, v_c, nc, nc)
            c1, hxc, ridx_c, ridxt_c = _merge_ln_route(
                o_c, zc_c, bg_c, gs_c, ctx, cm(g1), 1.0 + cm(sc2), cm(sh2), lw, nc)
            per_block = BLOCK_TOKENS // nc
            ridxt_c = ridxt_c.reshape(b // per_block, per_block, 8, nc).transpose(0, 2, 1, 3)
            hx_all = jnp.concatenate([hx_all, hxc.reshape(b * nc, XROW_W)], axis=0)
            ridx_all = jnp.concatenate([ridx_all, ridx_c.reshape(b * nc, LANES)], axis=0)
            ridxt = jnp.concatenate([ridxt, ridxt_c.reshape(b // per_block, 8, BLOCK_TOKENS)], axis=0)

        tabs = _route_tables(ridx_all)
        ntiles = _num_expert_tiles(ridx_all.shape[0])
        xs = _dispatch(tabs, ridxt, hx_all, ntiles * EXPERT_TILE)
        ys = _expert_ffn(tabs, xs, lw["w_gu"], lw["b_gu"], lw["w_down"], lw["b_down"], ntiles)
        x = _combine_ln(tabs, ys, ridx_all, x1.reshape(b * n, d), xm(g2), n,
                        lw["ln2_g"], lw["ln2_b"], 0).reshape(b, n, d)
        if update_ctx:
            ctx = _combine_ln(tabs, ys, ridx_all, c1.reshape(b * nc, d), cm(g2), nc,
                              lw["ln2_g"], lw["ln2_b"], (b * n) // BLOCK_TOKENS).reshape(b, nc, d)
    return x
```

```python
import functools

import jax
import jax.numpy as jnp
from jax import lax
from jax.experimental import pallas as pl
from jax.experimental.pallas import tpu as pltpu

F32 = jnp.float32
BF16 = jnp.bfloat16
I32 = jnp.int32

D_MODEL = 1024
DEPTH = 2
GRID_W = 64
N_HEADS = 8
QK_NOPE = 64
QK_ROPE = 32
V_DIM = 64
Q_LORA = 256
KV_LORA = 128
ROPE_THETA = 10000.0
ATTN_SCALE = (QK_NOPE + QK_ROPE) ** -0.5
CONV_WIDTH = 512
CONV_K = 3
N_EXPERTS = 32
TOP_K = 4
D_FF = D_MODEL
SWIGLU_LIMIT = 7.0
SWIGLU_ALPHA = 1.702
NORM_EPS = 1e-6
DEEPNORM_ALPHA = (2 * DEPTH) ** 0.25

OFF_QA = 0
OFF_KV = OFF_QA + Q_LORA
OFF_KR = OFF_KV + KV_LORA
OFF_CX = OFF_KR + QK_ROPE
OFF_CB = OFF_CX + CONV_WIDTH
OFF_CC = OFF_CB + CONV_WIDTH
OFF_GA = OFF_CC + CONV_WIDTH
OFF_GB = OFF_GA + D_MODEL
IN_WIDTH = OFF_GB + D_MODEL

LANES = 128
QK_WIDTH = 2 * LANES
LOG2E = 1.4426950408889634
NEG_BIG = -1e30
VMEM_LIMIT = 56 * 1024 * 1024

PA_W = Q_LORA + KV_LORA + 2 * LANES
PC_OFF = PA_W
PC_W = 3 * CONV_WIDTH
PG_OFF = PC_OFF + PC_W
PG_W = 2 * D_MODEL
PACK_W = PG_OFF + PG_W

ROW_TILE = 512
Q_TILE = 2048
KV_CHUNK = 768
EXPERT_TILE = 512
BLOCK_TOKENS = 512
XROW_W = D_MODEL + LANES
STRIP_ROWS = 16
LOCAL_ROWS = TOP_K * BLOCK_TOKENS + N_EXPERTS * STRIP_ROWS
STRIP_BITS = 8


def _dot(a, b):
    return jnp.dot(a, b, preferred_element_type=F32)


def _split_bf16(a):
    hi = a.astype(BF16)
    lo = (a - hi.astype(F32)).astype(BF16)
    return hi, lo


def _dot3(a, b):
    ah, al = _split_bf16(a)
    bh, bl = _split_bf16(b)
    return _dot(ah, bh) + (_dot(ah, bl) + _dot(al, bh))


def _rms_norm(x, g):
    return x * lax.rsqrt(jnp.mean(jnp.square(x), axis=-1, keepdims=True) + NORM_EPS) * g


def _layer_norm(x, g, b):
    mu = jnp.mean(x, axis=-1, keepdims=True)
    xc = x - mu
    var = jnp.mean(jnp.square(xc), axis=-1, keepdims=True)
    return xc * lax.rsqrt(var + NORM_EPS) * g + b


def _const_spec(shape):
    nd = len(shape)
    return pl.BlockSpec(shape, lambda *_: (0,) * nd, pipeline_mode=pl.Buffered(1))


def _mod_kernel(c_ref, w_ref, b_ref, o_ref):
    c = c_ref[...]
    s = c * jax.nn.sigmoid(c)
    o_ref[0] = _dot3(s, w_ref[0]) + b_ref[0]


def _modulation(cvec, w_ada, b_ada):
    depth, d, n = w_ada.shape
    tn = 1536
    return pl.pallas_call(
        _mod_kernel,
        out_shape=jax.ShapeDtypeStruct((depth, 8, n), F32),
        grid=(depth, n // tn),
        in_specs=[
            pl.BlockSpec((8, d), lambda l, j: (0, 0)),
            pl.BlockSpec((1, d, tn), lambda l, j: (l, 0, j)),
            pl.BlockSpec((1, 1, tn), lambda l, j: (l, 0, j)),
        ],
        out_specs=pl.BlockSpec((1, 8, tn), lambda l, j: (l, 0, j)),
        compiler_params=pltpu.CompilerParams(
            dimension_semantics=("arbitrary", "arbitrary"), vmem_limit_bytes=VMEM_LIMIT),
        name="adaln_mod",
    )(cvec, w_ada, b_ada.reshape(depth, 1, n))


def _inproj_kernel(x_ref, scp_ref, sh_ref, cos_ref, sin_ref, w_ref, b_ref, qn_ref, kvn_ref,
                   wq_ref, bduk_ref,
                   q_ref, kt_ref, v_ref, zc_ref, bg_ref, gs_ref):
    h = (x_ref[0] * scp_ref[0] + sh_ref[0]).astype(BF16)
    cos = cos_ref[...]
    sin = sin_ref[...]

    pa = _dot(h, w_ref[:, 0:PA_W]) + b_ref[:, 0:PA_W]
    qn = _rms_norm(pa[:, 0:Q_LORA], qn_ref[...]).astype(BF16)
    ckv = _rms_norm(pa[:, Q_LORA:Q_LORA + KV_LORA], kvn_ref[...])
    kr = pa[:, Q_LORA + KV_LORA:Q_LORA + KV_LORA + LANES]
    krr = pa[:, Q_LORA + KV_LORA + LANES:PA_W]
    k_rope = kr * cos + krr * sin
    k_row = jnp.concatenate([ckv, k_rope], axis=1)
    kt_ref[0] = k_row.T.astype(BF16)
    ones_col = jnp.where(lax.broadcasted_iota(I32, ckv.shape, 1) == 0, 1.0, 0.0)
    v_ref[0] = jnp.concatenate([ckv, ones_col], axis=1).astype(BF16)

    nope_w = N_HEADS * QK_NOPE
    blk_w = N_HEADS * LANES
    q_all = _dot(qn, wq_ref[...])
    q_lat = _dot(q_all[:, 0:nope_w].astype(BF16), bduk_ref[...])
    qscale = ATTN_SCALE * LOG2E
    for hd in range(N_HEADS):
        lo = hd * LANES
        qr = q_all[:, nope_w + lo:nope_w + lo + LANES]
        qrr = q_all[:, nope_w + blk_w + lo:nope_w + blk_w + lo + LANES]
        q_rope = qr * cos + qrr * sin
        q_row = jnp.concatenate([q_lat[:, lo:lo + LANES], q_rope], axis=1) * qscale
        q_ref[0, hd] = q_row.astype(BF16)

    pc = _dot(h, w_ref[:, PC_OFF:PC_OFF + PC_W]) + b_ref[:, PC_OFF:PC_OFF + PC_W]
    zc_ref[0] = pc[:, 2 * CONV_WIDTH:3 * CONV_WIDTH] * pc[:, 0:CONV_WIDTH]
    bg_ref[0] = pc[:, CONV_WIDTH:2 * CONV_WIDTH]

    pg = _dot(h, w_ref[:, PG_OFF:PG_OFF + PG_W]) + b_ref[:, PG_OFF:PG_OFF + PG_W]
    gs_ref[0] = jax.nn.sigmoid(pg).astype(BF16)


def _in_projection(x, scp, sh, cos, sin, lw, tm):
    b, s, d = x.shape
    row = lambda bi, i: (bi, i, 0)
    per_b = lambda bi, i: (bi, 0, 0)
    return pl.pallas_call(
        _inproj_kernel,
        out_shape=(
            jax.ShapeDtypeStruct((b, N_HEADS, s, QK_WIDTH), BF16),
            jax.ShapeDtypeStruct((b, QK_WIDTH, s), BF16),
            jax.ShapeDtypeStruct((b, s, QK_WIDTH), BF16),
            jax.ShapeDtypeStruct((b, s, CONV_WIDTH), F32),
            jax.ShapeDtypeStruct((b, s, CONV_WIDTH), F32),
            jax.ShapeDtypeStruct((b, s, 2 * D_MODEL), BF16),
        ),
        grid=(b, s // tm),
        in_specs=[
            pl.BlockSpec((1, tm, d), row),
            pl.BlockSpec((1, 1, d), per_b),
            pl.BlockSpec((1, 1, d), per_b),
            pl.BlockSpec((tm, LANES), lambda bi, i: (i, 0)),
            pl.BlockSpec((tm, LANES), lambda bi, i: (i, 0)),
            _const_spec((d, PACK_W)),
            _const_spec((1, PACK_W)),
            _const_spec((1, Q_LORA)),
            _const_spec((1, KV_LORA)),
            _const_spec(lw["wq"].shape),
            _const_spec(lw["bduk"].shape),
        ],
        out_specs=(
            pl.BlockSpec((1, N_HEADS, tm, QK_WIDTH), lambda bi, i: (bi, 0, i, 0)),
            pl.BlockSpec((1, QK_WIDTH, tm), lambda bi, i: (bi, 0, i)),
            pl.BlockSpec((1, tm, QK_WIDTH), row),
            pl.BlockSpec((1, tm, CONV_WIDTH), row),
            pl.BlockSpec((1, tm, CONV_WIDTH), row),
            pl.BlockSpec((1, tm, 2 * D_MODEL), row),
        ),
        compiler_params=pltpu.CompilerParams(
            dimension_semantics=("arbitrary", "arbitrary"), vmem_limit_bytes=VMEM_LIMIT),
        name="in_projection",
    )(x, scp, sh, cos, sin, lw["w_pack"], lw["b_pack"], lw["q_norm"], lw["kv_norm"],
      lw["wq"], lw["bduk"])


def _attn_kernel(q_ref, kt_ref, v_ref, o_ref, s0_ref, s1_ref, m_ref, acc_ref, *, tk, nchunks):
    q = q_ref[0, 0]

    def scores(c, s_ref):
        off = pl.multiple_of(c * tk, tk)
        s_ref[...] = _dot(q, kt_ref[0, :, pl.ds(off, tk)])

    def absorb(c, s_ref):
        off = pl.multiple_of(c * tk, tk)
        tiles = [s_ref[:, j * LANES:(j + 1) * LANES] for j in range(tk // LANES)]
        m_tile = functools.reduce(jnp.maximum, tiles)
        m_old = m_ref[...]
        m_new = jnp.maximum(m_old, jnp.broadcast_to(jnp.max(m_tile, axis=-1, keepdims=True),
                                                    m_old.shape))
        m_ref[...] = m_new
        p = jnp.concatenate([jnp.exp2((t - m_new).astype(BF16)) for t in tiles], axis=1)
        alpha = jnp.exp2(m_old - m_new)
        pv = _dot(p, v_ref[0, pl.ds(off, tk), :])
        acc_ref[...] = jnp.concatenate([alpha, alpha], axis=1) * acc_ref[...] + pv

    m_ref[...] = jnp.full(m_ref.shape, NEG_BIG, F32)
    acc_ref[...] = jnp.zeros(acc_ref.shape, F32)
    scores(0, s0_ref)
    npairs = (nchunks - 1) // 2

    def pair(i, carry):
        c = 2 * i
        scores(c + 1, s1_ref)
        absorb(c, s0_ref)
        scores(c + 2, s0_ref)
        absorb(c + 1, s1_ref)
        return carry

    lax.fori_loop(0, npairs, pair, 0)
    if nchunks % 2 == 0:
        scores(nchunks - 1, s1_ref)
        absorb(nchunks - 2, s0_ref)
        absorb(nchunks - 1, s1_ref)
    else:
        absorb(nchunks - 1, s0_ref)
    acc = acc_ref[...]
    o_ref[0] = (acc[:, 0:KV_LORA] / acc[:, KV_LORA:KV_LORA + 1]).astype(BF16)


def _latent_attention(q, kt, v, tq, tk):
    b, nh, s, _ = q.shape
    skv = kt.shape[2]
    return pl.pallas_call(
        functools.partial(_attn_kernel, tk=tk, nchunks=skv // tk),
        out_shape=jax.ShapeDtypeStruct((b, s, nh * KV_LORA), BF16),
        grid=(b, nh, s // tq),
        in_specs=[
            pl.BlockSpec((1, 1, tq, QK_WIDTH), lambda bi, hi, i: (bi, hi, i, 0)),
            pl.BlockSpec((1, QK_WIDTH, skv), lambda bi, hi, i: (bi, 0, 0)),
            pl.BlockSpec((1, skv, QK_WIDTH), lambda bi, hi, i: (bi, 0, 0)),
        ],
        out_specs=pl.BlockSpec((1, tq, KV_LORA), lambda bi, hi, i: (bi, i, hi)),
        scratch_shapes=[
            pltpu.VMEM((tq, tk), F32),
            pltpu.VMEM((tq, tk), F32),
            pltpu.VMEM((tq, LANES), F32),
            pltpu.VMEM((tq, QK_WIDTH), F32),
        ],
        compiler_params=pltpu.CompilerParams(
            dimension_semantics=("arbitrary", "arbitrary", "arbitrary"),
            vmem_limit_bytes=VMEM_LIMIT),
        name="latent_attention",
    )(q, kt, v)


def _merge_kernel(o_ref, zc_ref, zp_ref, zn_ref, bg_ref, gs_ref, x_ref, g1_ref, scp2_ref, sh2_ref,
                  bduv_ref, woa_ref, wob_ref, wo_ref, cw_ref, cb_ref, lng_ref, lnb_ref,
                  wr_ref, br_ref,
                  x1_ref, hx_ref, ridx_ref, ridxt_ref, *, ntiles):
    i = pl.program_id(1)
    tm = x_ref.shape[1]

    attn = _dot(o_ref[0], bduv_ref[...]).astype(BF16)
    y_a = _dot(attn, woa_ref[...])

    zc = zc_ref[0]
    rows = lax.broadcasted_iota(I32, zc.shape, 0)
    prev_row = jnp.where(i > 0, zp_ref[0][7:8, :], 0.0)
    next_row = jnp.where(i < ntiles - 1, zn_ref[0][0:1, :], 0.0)
    z_prev = jnp.where(rows == 0, prev_row, pltpu.roll(zc, 1, axis=0))
    z_next = jnp.where(rows == tm - 1, next_row, pltpu.roll(zc, tm - 1, axis=0))
    cw = cw_ref[...]
    conv = bg_ref[0] * (z_prev * cw[0:1] + zc * cw[1:2] + z_next * cw[2:3] + cb_ref[...])
    y_b = _dot(conv.astype(BF16), wob_ref[...])

    gs = gs_ref[0].astype(F32)
    merged = gs[:, 0:D_MODEL] * y_a + gs[:, D_MODEL:2 * D_MODEL] * y_b
    y = _dot(merged.astype(BF16), wo_ref[...])

    x1 = _layer_norm(DEEPNORM_ALPHA * x_ref[0] + g1_ref[0] * y, lng_ref[...], lnb_ref[...])
    x1_ref[0] = x1
    h2 = x1 * scp2_ref[0] + sh2_ref[0]

    logits = _dot3(h2, wr_ref[...]) + br_ref[...]
    lanes = lax.broadcasted_iota(I32, logits.shape, 1).astype(F32)
    sels, exps = [], []
    top = None
    denom = None
    for k in range(TOP_K):
        mx = jnp.max(logits, axis=-1, keepdims=True)
        sel = jnp.min(jnp.where(logits == mx, lanes, float(LANES)), axis=-1, keepdims=True)
        if k == 0:
            top = mx
            e = jnp.ones_like(mx)
            denom = e
        else:
            e = jnp.exp(mx - top)
            denom = denom + e
        sels.append(sel)
        exps.append(e)
        logits = jnp.where(lanes == sel, NEG_BIG * 2.0, logits)

    idx_out = jnp.zeros(logits.shape, F32)
    meta = jnp.zeros(logits.shape, F32)
    for k in range(TOP_K):
        g = exps[k] / denom
        hi = g.astype(BF16).astype(F32)
        r1 = g - hi
        mid = r1.astype(BF16).astype(F32)
        lo = (r1 - mid).astype(BF16).astype(F32)
        idx_out = jnp.where(lanes == k, sels[k], idx_out)
        meta = jnp.where(lanes == TOP_K + k, hi, meta)
        meta = jnp.where(lanes == 2 * TOP_K + k, mid, meta)
        meta = jnp.where(lanes == 3 * TOP_K + k, lo, meta)
    meta = meta + idx_out
    ridx_ref[0] = idx_out.astype(I32)
    ridxt_ref[0] = idx_out.T[0:8, :].astype(I32)
    hx_ref[0] = jnp.concatenate([h2.astype(BF16), meta.astype(BF16)], axis=1)


def _merge_ln_route(o, zc, bg, gs, x, g1, scp2, sh2, lw, tm):
    b, s, d = x.shape
    ntiles = s // tm
    h8 = tm // 8
    row = lambda bi, i: (bi, i, 0)
    per_b = lambda bi, i: (bi, 0, 0)
    consts = [lw["bduv"], lw["w_oa"], lw["w_ob"], lw["w_o"], lw["conv_w"], lw["conv_b"],
              lw["ln1_g"], lw["ln1_b"], lw["w_router"], lw["b_router"]]
    return pl.pallas_call(
        functools.partial(_merge_kernel, ntiles=ntiles),
        out_shape=(
            jax.ShapeDtypeStruct((b, s, d), F32),
            jax.ShapeDtypeStruct((b, s, XROW_W), BF16),
            jax.ShapeDtypeStruct((b, s, LANES), I32),
            jax.ShapeDtypeStruct((b * ntiles, 8, tm), I32),
        ),
        grid=(b, ntiles),
        in_specs=[
            pl.BlockSpec((1, tm, N_HEADS * KV_LORA), row),
            pl.BlockSpec((1, tm, CONV_WIDTH), row),
            pl.BlockSpec((1, 8, CONV_WIDTH), lambda bi, i: (bi, jnp.maximum(i * h8 - 1, 0), 0)),
            pl.BlockSpec((1, 8, CONV_WIDTH),
                         lambda bi, i: (bi, jnp.minimum((i + 1) * h8, s // 8 - 1), 0)),
            pl.BlockSpec((1, tm, CONV_WIDTH), row),
            pl.BlockSpec((1, tm, 2 * D_MODEL), row),
            pl.BlockSpec((1, tm, d), row),
            pl.BlockSpec((1, 1, d), per_b),
            pl.BlockSpec((1, 1, d), per_b),
            pl.BlockSpec((1, 1, d), per_b),
        ] + [_const_spec(w.shape) for w in consts],
        out_specs=(
            pl.BlockSpec((1, tm, d), row),
            pl.BlockSpec((1, tm, XROW_W), row),
            pl.BlockSpec((1, tm, LANES), row),
            pl.BlockSpec((1, 8, tm), lambda bi, i: (bi * ntiles + i, 0, 0)),
        ),
        compiler_params=pltpu.CompilerParams(
            dimension_semantics=("arbitrary", "arbitrary"), vmem_limit_bytes=VMEM_LIMIT),
        name="merge_ln_route",
    )(o, zc, zc, zc, bg, gs, x, g1, scp2, sh2, *consts)


def _strip_copies(units, src_ref, src_row, dst_ref, dst_row, sem):
    for bit in reversed(range(STRIP_BITS)):
        done = ((units >> (bit + 1)) << (bit + 1)) * STRIP_ROWS

        @pl.when(((units >> bit) & 1) == 1)
        def _(bit=bit, done=done):
            s = pl.multiple_of(src_row + done, STRIP_ROWS)
            d = pl.multiple_of(dst_row + done, STRIP_ROWS)
            size = STRIP_ROWS << bit
            pltpu.make_async_copy(src_ref.at[pl.ds(s, size)], dst_ref.at[pl.ds(d, size)], sem).start()


def _dispatch_kernel(goff_ref, nu_ref, lst_ref, tot_ref, ridxt_ref, lcol_ref, hx_ref, zeros_hbm,
                     xs_hbm, buf, sem, *, dump_row):
    del zeros_hbm
    b = pl.program_id(0)
    tb = hx_ref.shape[0]
    ridxt = ridxt_ref[0]
    e_iota = lax.broadcasted_iota(I32, (N_EXPERTS, tb), 0)
    hits = [ridxt[k:k + 1, :] == e_iota for k in range(TOP_K)]
    mt = sum(jnp.where(h, 1.0, 0.0) for h in hits)
    earlier = lax.broadcasted_iota(I32, (tb, tb), 0) < lax.broadcasted_iota(I32, (tb, tb), 1)
    rank = _dot(mt.astype(BF16), jnp.where(earlier, 1.0, 0.0).astype(BF16))
    base = lcol_ref[0] + rank
    slot = [jnp.sum(jnp.where(h, base, 0.0), axis=0, keepdims=True) for h in hits]

    j = lax.broadcasted_iota(I32, (LOCAL_ROWS, tb), 0).astype(F32)
    sel = jnp.where(j == slot[0], 1.0,
                    jnp.where(j == slot[1], 1.0,
                              jnp.where(j == slot[2], 1.0, jnp.where(j == slot[3], 1.0, 0.0))))
    buf[...] = _dot(sel.astype(BF16), hx_ref[...]).astype(BF16)

    for e in range(N_EXPERTS):
        t = b * N_EXPERTS + e
        _strip_copies(nu_ref[t], buf, lst_ref[t], xs_hbm, goff_ref[t], sem)
    tot = tot_ref[b]
    _strip_copies(LOCAL_ROWS // STRIP_ROWS - tot, buf, tot * STRIP_ROWS, xs_hbm,
                  dump_row + tot * STRIP_ROWS, sem)
    pltpu.make_async_copy(buf, xs_hbm.at[pl.ds(0, LOCAL_ROWS)], sem).wait()


def _dispatch(tabs, ridxt, hx, nslots):
    t, w = hx.shape
    tb = BLOCK_TOKENS
    nb = t // tb
    grid_spec = pltpu.PrefetchScalarGridSpec(
        num_scalar_prefetch=4,
        grid=(nb,),
        in_specs=[
            pl.BlockSpec((1, 8, tb), lambda i, *_: (i, 0, 0)),
            pl.BlockSpec((1, N_EXPERTS, 1), lambda i, *_: (i, 0, 0)),
            pl.BlockSpec((tb, w), lambda i, *_: (i, 0)),
            pl.BlockSpec(memory_space=pl.ANY),
        ],
        out_specs=pl.BlockSpec(memory_space=pl.ANY),
        scratch_shapes=[pltpu.VMEM((LOCAL_ROWS, w), BF16), pltpu.SemaphoreType.DMA(())],
    )
    out_shape = jax.ShapeDtypeStruct((nslots + LOCAL_ROWS, w), BF16)
    return pl.pallas_call(
        functools.partial(_dispatch_kernel, dump_row=nslots),
        out_shape=out_shape,
        grid_spec=grid_spec,
        input_output_aliases={7: 0},
        compiler_params=pltpu.CompilerParams(
            dimension_semantics=("arbitrary",), vmem_limit_bytes=VMEM_LIMIT),
        name="moe_dispatch",
    )(tabs["goff"], tabs["nu"], tabs["lst"], tabs["tot"], ridxt, tabs["lcol"], hx,
      jnp.zeros(out_shape.shape, out_shape.dtype))


def _expert_kernel(te_ref, nv_ref, first_ref, x_ref, wgu_ref, bgu_ref, wd_ref, bd_ref, y_ref,
                   wgu_bf, wd_bf):
    j = pl.program_id(0)
    nv = nv_ref[j]

    @pl.when(first_ref[j] == 1)
    def _():
        wgu_bf[...] = wgu_ref[0].astype(BF16)
        wd_bf[...] = wd_ref[0].astype(BF16)

    @pl.when(nv > 0)
    def _():
        meta = x_ref[:, D_MODEL:XROW_W].astype(F32)
        lanes = lax.broadcasted_iota(I32, meta.shape, 1)
        gates = (pltpu.roll(meta, LANES - TOP_K, axis=1) + pltpu.roll(meta, LANES - 2 * TOP_K, axis=1)
                 + pltpu.roll(meta, LANES - 3 * TOP_K, axis=1))
        mine = (lanes < TOP_K) & (meta == te_ref[j].astype(F32))
        g = jnp.sum(jnp.where(mine, gates, 0.0), axis=-1, keepdims=True)

        gu = _dot(x_ref[:, 0:D_MODEL], wgu_bf[...]) + bgu_ref[0]
        gate = jnp.minimum(gu[:, 0:D_FF], SWIGLU_LIMIT)
        up = jnp.clip(gu[:, D_FF:2 * D_FF], -SWIGLU_LIMIT, SWIGLU_LIMIT)
        act = (up + 1.0) * gate * jax.nn.sigmoid(SWIGLU_ALPHA * gate)
        y_ref[...] = (g * (_dot(act.astype(BF16), wd_bf[...]) + bd_ref[0])).astype(BF16)

    @pl.when(nv <= 0)
    def _():
        y_ref[...] = jnp.zeros_like(y_ref)


def _expert_ffn(tabs, xs, layer, wgu, bgu, wd, bd, ntiles):
    tme = EXPERT_TILE
    w = xs.shape[1]
    nl, ne, d, f2 = wgu.shape
    f = wd.shape[2]
    base = layer * ne
    wmap = lambda j, te, nv, first: (base + te[j], 0, 0)
    grid_spec = pltpu.PrefetchScalarGridSpec(
        num_scalar_prefetch=3,
        grid=(ntiles,),
        in_specs=[
            pl.BlockSpec((tme, w), lambda j, te, nv, first: (j, 0)),
            pl.BlockSpec((1, d, f2), wmap),
            pl.BlockSpec((1, 1, f2), wmap),
            pl.BlockSpec((1, f, d), wmap),
            pl.BlockSpec((1, 1, d), wmap),
        ],
        out_specs=pl.BlockSpec((tme, d), lambda j, te, nv, first: (j, 0)),
        scratch_shapes=[pltpu.VMEM((d, f2), BF16), pltpu.VMEM((f, d), BF16)],
    )
    return pl.pallas_call(
        _expert_kernel,
        out_shape=jax.ShapeDtypeStruct((ntiles * tme, d), BF16),
        grid_spec=grid_spec,
        compiler_params=pltpu.CompilerParams(
            dimension_semantics=("arbitrary",), vmem_limit_bytes=VMEM_LIMIT),
        name="expert_ffn",
    )(tabs["te"], tabs["nv"], tabs["first"], xs, wgu.reshape(nl * ne, d, f2),
      bgu.reshape(nl * ne, 1, f2), wd.reshape(nl * ne, f, d), bd.reshape(nl * ne, 1, d))


def _combine_kernel(goff_ref, nu_ref, lst_ref, tot_ref, ys_hbm, ridx_ref, lrow_ref, x1_ref, g2_ref,
                    lng_ref, lnb_ref, o_ref, ybuf, sem, *, block0):
    b = pl.program_id(0) + block0
    tb = x1_ref.shape[0]
    for e in range(N_EXPERTS):
        t = b * N_EXPERTS + e
        _strip_copies(nu_ref[t], ys_hbm, goff_ref[t], ybuf, lst_ref[t], sem)
    tot = tot_ref[b]
    _strip_copies(LOCAL_ROWS // STRIP_ROWS - tot, ys_hbm, 0, ybuf, tot * STRIP_ROWS, sem)

    ridx = ridx_ref[...]
    lanes = lax.broadcasted_iota(I32, ridx.shape, 1)
    hits = [ridx[:, k:k + 1] == lanes for k in range(TOP_K)]
    m = sum(jnp.where(h, 1.0, 0.0) for h in hits)
    earlier = lax.broadcasted_iota(I32, (tb, tb), 1) < lax.broadcasted_iota(I32, (tb, tb), 0)
    rank = _dot(jnp.where(earlier, 1.0, 0.0).astype(BF16), m.astype(BF16))
    base = lrow_ref[0] + rank
    slot = [jnp.sum(jnp.where(h, base, 0.0), axis=-1, keepdims=True) for h in hits]
    j = lax.broadcasted_iota(I32, (tb, LOCAL_ROWS), 1).astype(F32)
    sel = jnp.where(j == slot[0], 1.0,
                    jnp.where(j == slot[1], 1.0,
                              jnp.where(j == slot[2], 1.0, jnp.where(j == slot[3], 1.0, 0.0))))
    sel = sel.astype(BF16)

    pltpu.make_async_copy(ys_hbm.at[pl.ds(0, LOCAL_ROWS)], ybuf, sem).wait()
    f = _dot(sel, ybuf[...])
    r = DEEPNORM_ALPHA * x1_ref[...] + g2_ref[0] * f
    o_ref[...] = _layer_norm(r, lng_ref[...], lnb_ref[...])


def _combine_ln(tabs, ys, ridx_all, x1, g2, rows_per_mod, ln_g, ln_b, block0):
    t, d = x1.shape
    tb = BLOCK_TOKENS
    grid_spec = pltpu.PrefetchScalarGridSpec(
        num_scalar_prefetch=4,
        grid=(t // tb,),
        in_specs=[
            pl.BlockSpec(memory_space=pl.ANY),
            pl.BlockSpec((tb, LANES), lambda i, *_: (i + block0, 0)),
            pl.BlockSpec((1, 1, LANES), lambda i, *_: (i + block0, 0, 0)),
            pl.BlockSpec((tb, d), lambda i, *_: (i, 0)),
            pl.BlockSpec((1, 1, d), lambda i, *_: ((i * tb) // rows_per_mod, 0, 0)),
            pl.BlockSpec((1, d), lambda i, *_: (0, 0)),
            pl.BlockSpec((1, d), lambda i, *_: (0, 0)),
        ],
        out_specs=pl.BlockSpec((tb, d), lambda i, *_: (i, 0)),
        scratch_shapes=[pltpu.VMEM((LOCAL_ROWS, d), BF16), pltpu.SemaphoreType.DMA(())],
    )
    return pl.pallas_call(
        functools.partial(_combine_kernel, block0=block0),
        out_shape=jax.ShapeDtypeStruct((t, d), F32),
        grid_spec=grid_spec,
        compiler_params=pltpu.CompilerParams(
            dimension_semantics=("arbitrary",), vmem_limit_bytes=VMEM_LIMIT),
        name="moe_combine_ln",
    )(tabs["goff"], tabs["nu"], tabs["lst"], tabs["tot"], ys, ridx_all, tabs["lrow"], x1, g2,
      ln_g, ln_b)


def _num_expert_tiles(t):
    nb = t // BLOCK_TOKENS
    return -(-(TOP_K * t + nb * N_EXPERTS * (STRIP_ROWS - 1)) // EXPERT_TILE) + N_EXPERTS


def _route_tables(ridx_all):
    t = ridx_all.shape[0]
    tb, tme = BLOCK_TOKENS, EXPERT_TILE
    nb = t // tb
    experts = jnp.arange(N_EXPERTS, dtype=I32)
    onehot = jnp.sum((ridx_all[:, :TOP_K, None] == experts[None, None, :]).astype(I32), axis=1)
    u = STRIP_ROWS
    nu = (jnp.sum(onehot.reshape(nb, tb, N_EXPERTS), axis=1) + u - 1) // u
    lstu = jnp.cumsum(nu, axis=1) - nu
    tot = jnp.sum(nu, axis=1)
    gu = jnp.sum(nu, axis=0)
    tu = tme // u
    tiles = (gu + tu - 1) // tu
    tile_end = jnp.cumsum(tiles)
    tile_first = tile_end - tiles
    goffu = (tile_first * tu)[None, :] + jnp.cumsum(nu, axis=0) - nu
    ntiles = _num_expert_tiles(t)
    jt = jnp.arange(ntiles, dtype=I32)
    te = jnp.minimum(jnp.sum((jt[:, None] >= tile_end[None, :]).astype(I32), axis=1), N_EXPERTS - 1)
    nv = jnp.clip(gu[te] * u - (jt - tile_first[te]) * tme, 0, tme)
    nv = jnp.where(jt < tile_end[-1], nv, 0)
    first = ((jt == tile_first[te]) & (nv > 0)).astype(I32)
    lst = (lstu * u).astype(I32)
    return dict(
        goff=(goffu * u).astype(I32).reshape(-1), nu=nu.astype(I32).reshape(-1), lst=lst.reshape(-1),
        tot=tot.astype(I32), te=te.astype(I32), nv=nv.astype(I32), first=first,
        lcol=lst.astype(F32)[:, :, None],
        lrow=_pad_lanes(lst.astype(F32))[:, None, :],
    )


def _rot_cols(w):
    q = QK_ROPE // 4
    w1, w2, w3, w4 = w[..., 0:q], w[..., q:2 * q], w[..., 2 * q:3 * q], w[..., 3 * q:4 * q]
    return jnp.concatenate([-w2, w1, -w4, w3], axis=-1)


def _pad_lanes(w, width=LANES):
    return jnp.pad(w, [(0, 0)] * (w.ndim - 1) + [(0, width - w.shape[-1])])


def _pack_layer(l, w_in, b_in, q_norm, kv_norm, w_uq, w_uk, w_uv, w_oa, conv_w, conv_b, w_ob, w_o,
                ln1_g, ln1_b, ln2_g, ln2_b, w_router, b_router):
    wi, bi = w_in[l], b_in[l][None, :]

    def pack(m):
        kr = m[:, OFF_KR:OFF_CX]
        return jnp.concatenate([
            m[:, OFF_QA:OFF_KR], _pad_lanes(kr), _pad_lanes(_rot_cols(kr)),
            m[:, OFF_CX:OFF_GA], m[:, OFF_GA:IN_WIDTH]], axis=1)

    uq = w_uq[l].reshape(Q_LORA, N_HEADS, QK_NOPE + QK_ROPE)
    uq_nope = uq[:, :, :QK_NOPE].reshape(Q_LORA, N_HEADS * QK_NOPE)
    uq_rope = uq[:, :, QK_NOPE:]
    wq = jnp.concatenate([
        uq_nope,
        _pad_lanes(uq_rope).reshape(Q_LORA, N_HEADS * LANES),
        _pad_lanes(_rot_cols(uq_rope)).reshape(Q_LORA, N_HEADS * LANES)], axis=1)

    eye = jnp.eye(N_HEADS, dtype=F32)
    uk = w_uk[l].reshape(KV_LORA, N_HEADS, QK_NOPE)
    bduk = jnp.einsum("chd,hg->hdgc", uk, eye).reshape(N_HEADS * QK_NOPE, N_HEADS * KV_LORA)
    uv = w_uv[l].reshape(KV_LORA, N_HEADS, V_DIM)
    bduv = jnp.einsum("chd,hg->hcgd", uv, eye).reshape(N_HEADS * KV_LORA, N_HEADS * V_DIM)

    return dict(
        w_pack=pack(wi).astype(BF16), b_pack=pack(bi),
        q_norm=q_norm[l][None, :], kv_norm=kv_norm[l][None, :],
        wq=wq.astype(BF16), bduk=bduk.astype(BF16), bduv=bduv.astype(BF16),
        w_oa=w_oa[l].astype(BF16), w_ob=w_ob[l].astype(BF16), w_o=w_o[l].astype(BF16),
        conv_w=jnp.pad(conv_w[l], ((0, 8 - CONV_K), (0, 0))), conv_b=conv_b[l][None, :],
        ln1_g=ln1_g[l][None, :], ln1_b=ln1_b[l][None, :],
        ln2_g=ln2_g[l][None, :], ln2_b=ln2_b[l][None, :],
        w_router=_pad_lanes(w_router[l]),
        b_router=jnp.pad(b_router[l], (0, LANES - N_EXPERTS), constant_values=NEG_BIG)[None, :],
    )


def _rope_tables(n):
    n_rows = n // GRID_W
    row = jnp.broadcast_to(jnp.arange(n_rows, dtype=F32)[:, None], (n_rows, GRID_W)).reshape(-1)
    col = jnp.broadcast_to(jnp.arange(GRID_W, dtype=F32)[None, :], (n_rows, GRID_W)).reshape(-1)
    half = QK_ROPE // 2
    freqs = ROPE_THETA ** (-jnp.arange(0, half, 2, dtype=F32) / half)
    ang_r = row[:, None] * freqs
    ang_c = col[:, None] * freqs
    ang = jnp.concatenate([ang_r, ang_r, ang_c, ang_c], axis=-1)
    return _pad_lanes(jnp.cos(ang)), _pad_lanes(jnp.sin(ang))


def kernel(x, c, ctx, c_ctx, w_ada, b_ada, w_in, b_in, q_norm, kv_norm, w_uq, w_uk, w_uv, w_oa,
           conv_w, conv_b, w_ob, w_o, ln1_g, ln1_b, ln2_g, ln2_b, w_router, b_router,
           w_gu, b_gu, w_down, b_down):
    b, n, d = x.shape
    nc = ctx.shape[1]
    assert ROW_TILE == BLOCK_TOKENS and BLOCK_TOKENS % nc == 0 and (b * nc) % BLOCK_TOKENS == 0
    cvec = jnp.concatenate([c, c_ctx[None, :], jnp.zeros((8 - b - 1, d), F32)], axis=0)
    mod = _modulation(cvec, w_ada, b_ada)

    cos_x, sin_x = _rope_tables(n)
    cos_c = _pad_lanes(jnp.ones((nc, QK_ROPE), F32))
    sin_c = jnp.zeros((nc, LANES), F32)

    for l in range(DEPTH):
        update_ctx = l < DEPTH - 1
        lw = _pack_layer(l, w_in, b_in, q_norm, kv_norm, w_uq, w_uk, w_uv, w_oa, conv_w, conv_b,
                         w_ob, w_o, ln1_g, ln1_b, ln2_g, ln2_b, w_router, b_router)
        sh1, sc1, g1, sh2, sc2, g2 = [m[:, None, :] for m in jnp.split(mod[l], 6, axis=-1)]
        xm = lambda m: m[:b]
        cm = lambda m: jnp.broadcast_to(m[b:b + 1], (b, 1, d))

        q_x, kt_x, v_x, zc_x, bg_x, gs_x = _in_projection(
            x, 1.0 + xm(sc1), xm(sh1), cos_x, sin_x, lw, ROW_TILE)
        q_c, kt_c, v_c, zc_c, bg_c, gs_c = _in_projection(
            ctx, 1.0 + cm(sc1), cm(sh1), cos_c, sin_c, lw, nc)
        kt_all = jnp.concatenate([kt_c, kt_x], axis=2)
        v_all = jnp.concatenate([v_c, v_x], axis=1)
        o_x = _latent_attention(q_x, kt_all, v_all, Q_TILE, KV_CHUNK)
        x1, hx, ridx, ridxt = _merge_ln_route(
            o_x, zc_x, bg_x, gs_x, x, xm(g1), 1.0 + xm(sc2), xm(sh2), lw, ROW_TILE)
        hx_all = hx.reshape(b * n, XROW_W)
        ridx_all = ridx.reshape(b * n, LANES)
        if update_ctx:
            o_c = _latent_attention(q_c, kt_c, v_c, nc, nc)
            c1, hxc, ridx_c, ridxt_c = _merge_ln_route(
                o_c, zc_c, bg_c, gs_c, ctx, cm(g1), 1.0 + cm(sc2), cm(sh2), lw, nc)
            per_block = BLOCK_TOKENS // nc
            ridxt_c = ridxt_c.reshape(b // per_block, per_block, 8, nc).transpose(0, 2, 1, 3)
            hx_all = jnp.concatenate([hx_all, hxc.reshape(b * nc, XROW_W)], axis=0)
            ridx_all = jnp.concatenate([ridx_all, ridx_c.reshape(b * nc, LANES)], axis=0)
            ridxt = jnp.concatenate([ridxt, ridxt_c.reshape(b // per_block, 8, BLOCK_TOKENS)], axis=0)

        tabs = _route_tables(ridx_all)
        ntiles = _num_expert_tiles(ridx_all.shape[0])
        xs = _dispatch(tabs, ridxt, hx_all, ntiles * EXPERT_TILE)
        ys = _expert_ffn(tabs, xs, l, w_gu, b_gu, w_down, b_down, ntiles)
        x = _combine_ln(tabs, ys, ridx_all, x1.reshape(b * n, d), xm(g2), n,
                        lw["ln2_g"], lw["ln2_b"], 0).reshape(b, n, d)
        if update_ctx:
            ctx = _combine_ln(tabs, ys, ridx_all, c1.reshape(b * nc, d), cm(g2), nc,
                              lw["ln2_g"], lw["ln2_b"], (b * n) // BLOCK_TOKENS).reshape(b, nc, d)
    return x
```

```python
import functools

import jax
import jax.numpy as jnp
from jax import lax
from jax.experimental import pallas as pl
from jax.experimental.pallas import tpu as pltpu

F32 = jnp.float32
BF16 = jnp.bfloat16
I32 = jnp.int32

D_MODEL = 1024
DEPTH = 2
GRID_W = 64
N_HEADS = 8
QK_NOPE = 64
QK_ROPE = 32
V_DIM = 64
Q_LORA = 256
KV_LORA = 128
ROPE_THETA = 10000.0
ATTN_SCALE = (QK_NOPE + QK_ROPE) ** -0.5
CONV_WIDTH = 512
CONV_K = 3
N_EXPERTS = 32
TOP_K = 4
D_FF = D_MODEL
SWIGLU_LIMIT = 7.0
SWIGLU_ALPHA = 1.702
NORM_EPS = 1e-6
DEEPNORM_ALPHA = (2 * DEPTH) ** 0.25

OFF_QA = 0
OFF_KV = OFF_QA + Q_LORA
OFF_KR = OFF_KV + KV_LORA
OFF_CX = OFF_KR + QK_ROPE
OFF_CB = OFF_CX + CONV_WIDTH
OFF_CC = OFF_CB + CONV_WIDTH
OFF_GA = OFF_CC + CONV_WIDTH
OFF_GB = OFF_GA + D_MODEL
IN_WIDTH = OFF_GB + D_MODEL

LANES = 128
QK_WIDTH = 2 * LANES
LOG2E = 1.4426950408889634
NEG_BIG = -1e30
VMEM_LIMIT = 56 * 1024 * 1024

PA_W = Q_LORA + KV_LORA + 2 * LANES
PC_OFF = PA_W
PC_W = 3 * CONV_WIDTH
PG_OFF = PC_OFF + PC_W
PG_W = 2 * D_MODEL
PACK_W = PG_OFF + PG_W

ROW_TILE = 512
Q_TILE = 2048
KV_CHUNK = 768
EXPERT_TILE = 1024
BLOCK_TOKENS = 512
XROW_W = D_MODEL + LANES
STRIP_ROWS = 16
LOCAL_ROWS = TOP_K * BLOCK_TOKENS + N_EXPERTS * STRIP_ROWS
STRIP_BITS = 8


def _dot(a, b):
    return jnp.dot(a, b, preferred_element_type=F32)


def _split_bf16(a):
    hi = a.astype(BF16)
    lo = (a - hi.astype(F32)).astype(BF16)
    return hi, lo


def _dot3(a, b):
    ah, al = _split_bf16(a)
    bh, bl = _split_bf16(b)
    return _dot(ah, bh) + (_dot(ah, bl) + _dot(al, bh))


def _rms_norm(x, g):
    return x * lax.rsqrt(jnp.mean(jnp.square(x), axis=-1, keepdims=True) + NORM_EPS) * g


def _layer_norm(x, g, b):
    mu = jnp.mean(x, axis=-1, keepdims=True)
    xc = x - mu
    var = jnp.mean(jnp.square(xc), axis=-1, keepdims=True)
    return xc * lax.rsqrt(var + NORM_EPS) * g + b


def _const_spec(shape):
    nd = len(shape)
    return pl.BlockSpec(shape, lambda *_: (0,) * nd, pipeline_mode=pl.Buffered(1))


def _mod_kernel(c_ref, w_ref, b_ref, o_ref):
    c = c_ref[...]
    s = c * jax.nn.sigmoid(c)
    o_ref[0] = _dot3(s, w_ref[0]) + b_ref[0]


def _modulation(cvec, w_ada, b_ada):
    depth, d, n = w_ada.shape
    tn = 1536
    return pl.pallas_call(
        _mod_kernel,
        out_shape=jax.ShapeDtypeStruct((depth, 8, n), F32),
        grid=(depth, n // tn),
        in_specs=[
            pl.BlockSpec((8, d), lambda l, j: (0, 0)),
            pl.BlockSpec((1, d, tn), lambda l, j: (l, 0, j)),
            pl.BlockSpec((1, 1, tn), lambda l, j: (l, 0, j)),
        ],
        out_specs=pl.BlockSpec((1, 8, tn), lambda l, j: (l, 0, j)),
        compiler_params=pltpu.CompilerParams(
            dimension_semantics=("arbitrary", "arbitrary"), vmem_limit_bytes=VMEM_LIMIT),
        name="adaln_mod",
    )(cvec, w_ada, b_ada.reshape(depth, 1, n))


def _inproj_kernel(x_ref, scp_ref, sh_ref, cos_ref, sin_ref, w_ref, b_ref, qn_ref, kvn_ref,
                   wq_ref, bduk_ref,
                   q_ref, kt_ref, v_ref, zc_ref, bg_ref, gs_ref):
    h = (x_ref[0] * scp_ref[0] + sh_ref[0]).astype(BF16)
    cos = cos_ref[...]
    sin = sin_ref[...]

    pa = _dot(h, w_ref[:, 0:PA_W]) + b_ref[:, 0:PA_W]
    qn = _rms_norm(pa[:, 0:Q_LORA], qn_ref[...]).astype(BF16)
    ckv = _rms_norm(pa[:, Q_LORA:Q_LORA + KV_LORA], kvn_ref[...])
    kr = pa[:, Q_LORA + KV_LORA:Q_LORA + KV_LORA + LANES]
    krr = pa[:, Q_LORA + KV_LORA + LANES:PA_W]
    k_rope = kr * cos + krr * sin
    k_row = jnp.concatenate([ckv, k_rope], axis=1)
    kt_ref[0] = k_row.T.astype(BF16)
    ones_col = jnp.where(lax.broadcasted_iota(I32, ckv.shape, 1) == 0, 1.0, 0.0)
    v_ref[0] = jnp.concatenate([ckv, ones_col], axis=1).astype(BF16)

    nope_w = N_HEADS * QK_NOPE
    blk_w = N_HEADS * LANES
    q_all = _dot(qn, wq_ref[...])
    q_lat = _dot(q_all[:, 0:nope_w].astype(BF16), bduk_ref[...])
    qscale = ATTN_SCALE * LOG2E
    for hd in range(N_HEADS):
        lo = hd * LANES
        qr = q_all[:, nope_w + lo:nope_w + lo + LANES]
        qrr = q_all[:, nope_w + blk_w + lo:nope_w + blk_w + lo + LANES]
        q_rope = qr * cos + qrr * sin
        q_row = jnp.concatenate([q_lat[:, lo:lo + LANES], q_rope], axis=1) * qscale
        q_ref[0, hd] = q_row.astype(BF16)

    pc = _dot(h, w_ref[:, PC_OFF:PC_OFF + PC_W]) + b_ref[:, PC_OFF:PC_OFF + PC_W]
    zc_ref[0] = pc[:, 2 * CONV_WIDTH:3 * CONV_WIDTH] * pc[:, 0:CONV_WIDTH]
    bg_ref[0] = pc[:, CONV_WIDTH:2 * CONV_WIDTH]

    pg = _dot(h, w_ref[:, PG_OFF:PG_OFF + PG_W]) + b_ref[:, PG_OFF:PG_OFF + PG_W]
    gs_ref[0] = jax.nn.sigmoid(pg).astype(BF16)


def _in_projection(x, scp, sh, cos, sin, lw, tm):
    b, s, d = x.shape
    row = lambda bi, i: (bi, i, 0)
    per_b = lambda bi, i: (bi, 0, 0)
    return pl.pallas_call(
        _inproj_kernel,
        out_shape=(
            jax.ShapeDtypeStruct((b, N_HEADS, s, QK_WIDTH), BF16),
            jax.ShapeDtypeStruct((b, QK_WIDTH, s), BF16),
            jax.ShapeDtypeStruct((b, s, QK_WIDTH), BF16),
            jax.ShapeDtypeStruct((b, s, CONV_WIDTH), F32),
            jax.ShapeDtypeStruct((b, s, CONV_WIDTH), F32),
            jax.ShapeDtypeStruct((b, s, 2 * D_MODEL), BF16),
        ),
        grid=(b, s // tm),
        in_specs=[
            pl.BlockSpec((1, tm, d), row),
            pl.BlockSpec((1, 1, d), per_b),
            pl.BlockSpec((1, 1, d), per_b),
            pl.BlockSpec((tm, LANES), lambda bi, i: (i, 0)),
            pl.BlockSpec((tm, LANES), lambda bi, i: (i, 0)),
            _const_spec((d, PACK_W)),
            _const_spec((1, PACK_W)),
            _const_spec((1, Q_LORA)),
            _const_spec((1, KV_LORA)),
            _const_spec(lw["wq"].shape),
            _const_spec(lw["bduk"].shape),
        ],
        out_specs=(
            pl.BlockSpec((1, N_HEADS, tm, QK_WIDTH), lambda bi, i: (bi, 0, i, 0)),
            pl.BlockSpec((1, QK_WIDTH, tm), lambda bi, i: (bi, 0, i)),
            pl.BlockSpec((1, tm, QK_WIDTH), row),
            pl.BlockSpec((1, tm, CONV_WIDTH), row),
            pl.BlockSpec((1, tm, CONV_WIDTH), row),
            pl.BlockSpec((1, tm, 2 * D_MODEL), row),
        ),
        compiler_params=pltpu.CompilerParams(
            dimension_semantics=("arbitrary", "arbitrary"), vmem_limit_bytes=VMEM_LIMIT),
        name="in_projection",
    )(x, scp, sh, cos, sin, lw["w_pack"], lw["b_pack"], lw["q_norm"], lw["kv_norm"],
      lw["wq"], lw["bduk"])


def _attn_kernel(q_ref, kt_ref, v_ref, o_ref, s0_ref, s1_ref, m_ref, acc_ref, *, tk, nchunks):
    q = q_ref[0, 0]

    def scores(c, s_ref):
        off = pl.multiple_of(c * tk, tk)
        s_ref[...] = _dot(q, kt_ref[0, :, pl.ds(off, tk)])

    def absorb(c, s_ref):
        off = pl.multiple_of(c * tk, tk)
        tiles = [s_ref[:, j * LANES:(j + 1) * LANES] for j in range(tk // LANES)]
        m_tile = functools.reduce(jnp.maximum, tiles)
        m_old = m_ref[...]
        m_new = jnp.maximum(m_old, jnp.broadcast_to(jnp.max(m_tile, axis=-1, keepdims=True),
                                                    m_old.shape))
        m_ref[...] = m_new
        p = jnp.concatenate([jnp.exp2((t - m_new).astype(BF16)) for t in tiles], axis=1)
        alpha = jnp.exp2(m_old - m_new)
        pv = _dot(p, v_ref[0, pl.ds(off, tk), :])
        acc_ref[...] = jnp.concatenate([alpha, alpha], axis=1) * acc_ref[...] + pv

    m_ref[...] = jnp.full(m_ref.shape, NEG_BIG, F32)
    acc_ref[...] = jnp.zeros(acc_ref.shape, F32)
    scores(0, s0_ref)
    npairs = (nchunks - 1) // 2

    def pair(i, carry):
        c = 2 * i
        scores(c + 1, s1_ref)
        absorb(c, s0_ref)
        scores(c + 2, s0_ref)
        absorb(c + 1, s1_ref)
        return carry

    lax.fori_loop(0, npairs, pair, 0)
    if nchunks % 2 == 0:
        scores(nchunks - 1, s1_ref)
        absorb(nchunks - 2, s0_ref)
        absorb(nchunks - 1, s1_ref)
    else:
        absorb(nchunks - 1, s0_ref)
    acc = acc_ref[...]
    o_ref[0] = (acc[:, 0:KV_LORA] / acc[:, KV_LORA:KV_LORA + 1]).astype(BF16)


def _latent_attention(q, kt, v, tq, tk):
    b, nh, s, _ = q.shape
    skv = kt.shape[2]
    return pl.pallas_call(
        functools.partial(_attn_kernel, tk=tk, nchunks=skv // tk),
        out_shape=jax.ShapeDtypeStruct((b, s, nh * KV_LORA), BF16),
        grid=(b, nh, s // tq),
        in_specs=[
            pl.BlockSpec((1, 1, tq, QK_WIDTH), lambda bi, hi, i: (bi, hi, i, 0)),
            pl.BlockSpec((1, QK_WIDTH, skv), lambda bi, hi, i: (bi, 0, 0)),
            pl.BlockSpec((1, skv, QK_WIDTH), lambda bi, hi, i: (bi, 0, 0)),
        ],
        out_specs=pl.BlockSpec((1, tq, KV_LORA), lambda bi, hi, i: (bi, i, hi)),
        scratch_shapes=[
            pltpu.VMEM((tq, tk), F32),
            pltpu.VMEM((tq, tk), F32),
            pltpu.VMEM((tq, LANES), F32),
            pltpu.VMEM((tq, QK_WIDTH), F32),
        ],
        compiler_params=pltpu.CompilerParams(
            dimension_semantics=("arbitrary", "arbitrary", "arbitrary"),
            vmem_limit_bytes=VMEM_LIMIT),
        name="latent_attention",
    )(q, kt, v)


def _merge_kernel(o_ref, zc_ref, zp_ref, zn_ref, bg_ref, gs_ref, x_ref, g1_ref, scp2_ref, sh2_ref,
                  bduv_ref, woa_ref, wob_ref, wo_ref, cw_ref, cb_ref, lng_ref, lnb_ref,
                  wr_ref, br_ref,
                  x1_ref, hx_ref, ridx_ref, ridxt_ref, *, ntiles):
    i = pl.program_id(1)
    tm = x_ref.shape[1]

    attn = _dot(o_ref[0], bduv_ref[...]).astype(BF16)
    y_a = _dot(attn, woa_ref[...])

    zc = zc_ref[0]
    rows = lax.broadcasted_iota(I32, zc.shape, 0)
    prev_row = jnp.where(i > 0, zp_ref[0][7:8, :], 0.0)
    next_row = jnp.where(i < ntiles - 1, zn_ref[0][0:1, :], 0.0)
    z_prev = jnp.where(rows == 0, prev_row, pltpu.roll(zc, 1, axis=0))
    z_next = jnp.where(rows == tm - 1, next_row, pltpu.roll(zc, tm - 1, axis=0))
    cw = cw_ref[...]
    conv = bg_ref[0] * (z_prev * cw[0:1] + zc * cw[1:2] + z_next * cw[2:3] + cb_ref[...])
    y_b = _dot(conv.astype(BF16), wob_ref[...])

    gs = gs_ref[0].astype(F32)
    merged = gs[:, 0:D_MODEL] * y_a + gs[:, D_MODEL:2 * D_MODEL] * y_b
    y = _dot(merged.astype(BF16), wo_ref[...])

    x1 = _layer_norm(DEEPNORM_ALPHA * x_ref[0] + g1_ref[0] * y, lng_ref[...], lnb_ref[...])
    x1_ref[0] = x1
    h2 = x1 * scp2_ref[0] + sh2_ref[0]

    logits = _dot3(h2, wr_ref[...]) + br_ref[...]
    lanes = lax.broadcasted_iota(I32, logits.shape, 1).astype(F32)
    sels, exps = [], []
    top = None
    denom = None
    for k in range(TOP_K):
        mx = jnp.max(logits, axis=-1, keepdims=True)
        sel = jnp.min(jnp.where(logits == mx, lanes, float(LANES)), axis=-1, keepdims=True)
        if k == 0:
            top = mx
            e = jnp.ones_like(mx)
            denom = e
        else:
            e = jnp.exp(mx - top)
            denom = denom + e
        sels.append(sel)
        exps.append(e)
        logits = jnp.where(lanes == sel, NEG_BIG * 2.0, logits)

    idx_out = jnp.zeros(logits.shape, F32)
    meta = jnp.zeros(logits.shape, F32)
    for k in range(TOP_K):
        g = exps[k] / denom
        hi = g.astype(BF16).astype(F32)
        r1 = g - hi
        mid = r1.astype(BF16).astype(F32)
        lo = (r1 - mid).astype(BF16).astype(F32)
        idx_out = jnp.where(lanes == k, sels[k], idx_out)
        meta = jnp.where(lanes == TOP_K + k, hi, meta)
        meta = jnp.where(lanes == 2 * TOP_K + k, mid, meta)
        meta = jnp.where(lanes == 3 * TOP_K + k, lo, meta)
    meta = meta + idx_out
    ridx_ref[0] = idx_out.astype(I32)
    ridxt_ref[0] = idx_out.T[0:8, :].astype(I32)
    hx_ref[0] = jnp.concatenate([h2.astype(BF16), meta.astype(BF16)], axis=1)


def _merge_ln_route(o, zc, bg, gs, x, g1, scp2, sh2, lw, tm):
    b, s, d = x.shape
    ntiles = s // tm
    h8 = tm // 8
    row = lambda bi, i: (bi, i, 0)
    per_b = lambda bi, i: (bi, 0, 0)
    consts = [lw["bduv"], lw["w_oa"], lw["w_ob"], lw["w_o"], lw["conv_w"], lw["conv_b"],
              lw["ln1_g"], lw["ln1_b"], lw["w_router"], lw["b_router"]]
    return pl.pallas_call(
        functools.partial(_merge_kernel, ntiles=ntiles),
        out_shape=(
            jax.ShapeDtypeStruct((b, s, d), F32),
            jax.ShapeDtypeStruct((b, s, XROW_W), BF16),
            jax.ShapeDtypeStruct((b, s, LANES), I32),
            jax.ShapeDtypeStruct((b * ntiles, 8, tm), I32),
        ),
        grid=(b, ntiles),
        in_specs=[
            pl.BlockSpec((1, tm, N_HEADS * KV_LORA), row),
            pl.BlockSpec((1, tm, CONV_WIDTH), row),
            pl.BlockSpec((1, 8, CONV_WIDTH), lambda bi, i: (bi, jnp.maximum(i * h8 - 1, 0), 0)),
            pl.BlockSpec((1, 8, CONV_WIDTH),
                         lambda bi, i: (bi, jnp.minimum((i + 1) * h8, s // 8 - 1), 0)),
            pl.BlockSpec((1, tm, CONV_WIDTH), row),
            pl.BlockSpec((1, tm, 2 * D_MODEL), row),
            pl.BlockSpec((1, tm, d), row),
            pl.BlockSpec((1, 1, d), per_b),
            pl.BlockSpec((1, 1, d), per_b),
            pl.BlockSpec((1, 1, d), per_b),
        ] + [_const_spec(w.shape) for w in consts],
        out_specs=(
            pl.BlockSpec((1, tm, d), row),
            pl.BlockSpec((1, tm, XROW_W), row),
            pl.BlockSpec((1, tm, LANES), row),
            pl.BlockSpec((1, 8, tm), lambda bi, i: (bi * ntiles + i, 0, 0)),
        ),
        compiler_params=pltpu.CompilerParams(
            dimension_semantics=("arbitrary", "arbitrary"), vmem_limit_bytes=VMEM_LIMIT),
        name="merge_ln_route",
    )(o, zc, zc, zc, bg, gs, x, g1, scp2, sh2, *consts)


def _strip_copies(units, src_ref, src_row, dst_ref, dst_row, sem):
    for bit in reversed(range(STRIP_BITS)):
        done = ((units >> (bit + 1)) << (bit + 1)) * STRIP_ROWS

        @pl.when(((units >> bit) & 1) == 1)
        def _(bit=bit, done=done):
            s = pl.multiple_of(src_row + done, STRIP_ROWS)
            d = pl.multiple_of(dst_row + done, STRIP_ROWS)
            size = STRIP_ROWS << bit
            pltpu.make_async_copy(src_ref.at[pl.ds(s, size)], dst_ref.at[pl.ds(d, size)], sem).start()


def _dispatch_kernel(goff_ref, nu_ref, lst_ref, tot_ref, ridxt_ref, lcol_ref, hx_ref, zeros_hbm,
                     xs_hbm, buf, sem, *, dump_row):
    del zeros_hbm
    b = pl.program_id(0)
    tb = hx_ref.shape[0]
    ridxt = ridxt_ref[0]
    e_iota = lax.broadcasted_iota(I32, (N_EXPERTS, tb), 0)
    hits = [ridxt[k:k + 1, :] == e_iota for k in range(TOP_K)]
    mt = sum(jnp.where(h, 1.0, 0.0) for h in hits)
    earlier = lax.broadcasted_iota(I32, (tb, tb), 0) < lax.broadcasted_iota(I32, (tb, tb), 1)
    rank = _dot(mt.astype(BF16), jnp.where(earlier, 1.0, 0.0).astype(BF16))
    base = lcol_ref[0] + rank
    slot = [jnp.sum(jnp.where(h, base, 0.0), axis=0, keepdims=True) for h in hits]

    j = lax.broadcasted_iota(I32, (LOCAL_ROWS, tb), 0).astype(F32)
    sel = jnp.where(j == slot[0], 1.0,
                    jnp.where(j == slot[1], 1.0,
                              jnp.where(j == slot[2], 1.0, jnp.where(j == slot[3], 1.0, 0.0))))
    buf[...] = _dot(sel.astype(BF16), hx_ref[...]).astype(BF16)

    for e in range(N_EXPERTS):
        t = b * N_EXPERTS + e
        _strip_copies(nu_ref[t], buf, lst_ref[t], xs_hbm, goff_ref[t], sem)
    tot = tot_ref[b]
    _strip_copies(LOCAL_ROWS // STRIP_ROWS - tot, buf, tot * STRIP_ROWS, xs_hbm,
                  dump_row + tot * STRIP_ROWS, sem)
    pltpu.make_async_copy(buf, xs_hbm.at[pl.ds(0, LOCAL_ROWS)], sem).wait()


def _dispatch(tabs, ridxt, hx, nslots):
    t, w = hx.shape
    tb = BLOCK_TOKENS
    nb = t // tb
    grid_spec = pltpu.PrefetchScalarGridSpec(
        num_scalar_prefetch=4,
        grid=(nb,),
        in_specs=[
            pl.BlockSpec((1, 8, tb), lambda i, *_: (i, 0, 0)),
            pl.BlockSpec((1, N_EXPERTS, 1), lambda i, *_: (i, 0, 0)),
            pl.BlockSpec((tb, w), lambda i, *_: (i, 0)),
            pl.BlockSpec(memory_space=pl.ANY),
        ],
        out_specs=pl.BlockSpec(memory_space=pl.ANY),
        scratch_shapes=[pltpu.VMEM((LOCAL_ROWS, w), BF16), pltpu.SemaphoreType.DMA(())],
    )
    out_shape = jax.ShapeDtypeStruct((nslots + LOCAL_ROWS, w), BF16)
    return pl.pallas_call(
        functools.partial(_dispatch_kernel, dump_row=nslots),
        out_shape=out_shape,
        grid_spec=grid_spec,
        input_output_aliases={7: 0},
        compiler_params=pltpu.CompilerParams(
            dimension_semantics=("arbitrary",), vmem_limit_bytes=VMEM_LIMIT),
        name="moe_dispatch",
    )(tabs["goff"], tabs["nu"], tabs["lst"], tabs["tot"], ridxt, tabs["lcol"], hx,
      jnp.zeros(out_shape.shape, out_shape.dtype))


def _expert_kernel(te_ref, nv_ref, first_ref, x_ref, wgu_ref, bgu_ref, wd_ref, bd_ref, y_ref,
                   wgu_bf, wd_bf):
    j = pl.program_id(0)
    nv = nv_ref[j]

    @pl.when(first_ref[j] == 1)
    def _():
        wgu_bf[...] = wgu_ref[0].astype(BF16)
        wd_bf[...] = wd_ref[0].astype(BF16)

    @pl.when(nv > 0)
    def _():
        meta = x_ref[:, D_MODEL:XROW_W].astype(F32)
        lanes = lax.broadcasted_iota(I32, meta.shape, 1)
        gates = (pltpu.roll(meta, LANES - TOP_K, axis=1) + pltpu.roll(meta, LANES - 2 * TOP_K, axis=1)
                 + pltpu.roll(meta, LANES - 3 * TOP_K, axis=1))
        mine = (lanes < TOP_K) & (meta == te_ref[j].astype(F32))
        g = jnp.sum(jnp.where(mine, gates, 0.0), axis=-1, keepdims=True)

        gu = _dot(x_ref[:, 0:D_MODEL], wgu_bf[...]) + bgu_ref[0]
        gate = jnp.minimum(gu[:, 0:D_FF], SWIGLU_LIMIT)
        up = jnp.clip(gu[:, D_FF:2 * D_FF], -SWIGLU_LIMIT, SWIGLU_LIMIT)
        act = (up + 1.0) * gate * jax.nn.sigmoid(SWIGLU_ALPHA * gate)
        y_ref[...] = (g * (_dot(act.astype(BF16), wd_bf[...]) + bd_ref[0])).astype(BF16)

    @pl.when(nv <= 0)
    def _():
        y_ref[...] = jnp.zeros_like(y_ref)


def _expert_ffn(tabs, xs, layer, wgu, bgu, wd, bd, ntiles):
    tme = EXPERT_TILE
    w = xs.shape[1]
    nl, ne, d, f2 = wgu.shape
    f = wd.shape[2]
    base = layer * ne
    wmap = lambda j, te, nv, first: (base + te[j], 0, 0)
    grid_spec = pltpu.PrefetchScalarGridSpec(
        num_scalar_prefetch=3,
        grid=(ntiles,),
        in_specs=[
            pl.BlockSpec((tme, w), lambda j, te, nv, first: (j, 0)),
            pl.BlockSpec((1, d, f2), wmap),
            pl.BlockSpec((1, 1, f2), wmap),
            pl.BlockSpec((1, f, d), wmap),
            pl.BlockSpec((1, 1, d), wmap),
        ],
        out_specs=pl.BlockSpec((tme, d), lambda j, te, nv, first: (j, 0)),
        scratch_shapes=[pltpu.VMEM((d, f2), BF16), pltpu.VMEM((f, d), BF16)],
    )
    return pl.pallas_call(
        _expert_kernel,
        out_shape=jax.ShapeDtypeStruct((ntiles * tme, d), BF16),
        grid_spec=grid_spec,
        compiler_params=pltpu.CompilerParams(
            dimension_semantics=("arbitrary",), vmem_limit_bytes=VMEM_LIMIT),
        name="expert_ffn",
    )(tabs["te"], tabs["nv"], tabs["first"], xs, wgu.reshape(nl * ne, d, f2),
      bgu.reshape(nl * ne, 1, f2), wd.reshape(nl * ne, f, d), bd.reshape(nl * ne, 1, d))


def _combine_kernel(goff_ref, nu_ref, lst_ref, tot_ref, ys_hbm, ridx_ref, lrow_ref, x1_ref, g2_ref,
                    lng_ref, lnb_ref, o_ref, ybuf, sem, *, block0):
    b = pl.program_id(0) + block0
    tb = x1_ref.shape[0]
    for e in range(N_EXPERTS):
        t = b * N_EXPERTS + e
        _strip_copies(nu_ref[t], ys_hbm, goff_ref[t], ybuf, lst_ref[t], sem)
    tot = tot_ref[b]
    _strip_copies(LOCAL_ROWS // STRIP_ROWS - tot, ys_hbm, 0, ybuf, tot * STRIP_ROWS, sem)

    ridx = ridx_ref[...]
    lanes = lax.broadcasted_iota(I32, ridx.shape, 1)
    hits = [ridx[:, k:k + 1] == lanes for k in range(TOP_K)]
    m = sum(jnp.where(h, 1.0, 0.0) for h in hits)
    earlier = lax.broadcasted_iota(I32, (tb, tb), 1) < lax.broadcasted_iota(I32, (tb, tb), 0)
    rank = _dot(jnp.where(earlier, 1.0, 0.0).astype(BF16), m.astype(BF16))
    base = lrow_ref[0] + rank
    slot = [jnp.sum(jnp.where(h, base, 0.0), axis=-1, keepdims=True) for h in hits]
    j = lax.broadcasted_iota(I32, (tb, LOCAL_ROWS), 1).astype(F32)
    sel = jnp.where(j == slot[0], 1.0,
                    jnp.where(j == slot[1], 1.0,
                              jnp.where(j == slot[2], 1.0, jnp.where(j == slot[3], 1.0, 0.0))))
    sel = sel.astype(BF16)

    pltpu.make_async_copy(ys_hbm.at[pl.ds(0, LOCAL_ROWS)], ybuf, sem).wait()
    f = _dot(sel, ybuf[...])
    r = DEEPNORM_ALPHA * x1_ref[...] + g2_ref[0] * f
    o_ref[...] = _layer_norm(r, lng_ref[...], lnb_ref[...])


def _combine_ln(tabs, ys, ridx_all, x1, g2, rows_per_mod, ln_g, ln_b, block0):
    t, d = x1.shape
    tb = BLOCK_TOKENS
    grid_spec = pltpu.PrefetchScalarGridSpec(
        num_scalar_prefetch=4,
        grid=(t // tb,),
        in_specs=[
            pl.BlockSpec(memory_space=pl.ANY),
            pl.BlockSpec((tb, LANES), lambda i, *_: (i + block0, 0)),
            pl.BlockSpec((1, 1, LANES), lambda i, *_: (i + block0, 0, 0)),
            pl.BlockSpec((tb, d), lambda i, *_: (i, 0)),
            pl.BlockSpec((1, 1, d), lambda i, *_: ((i * tb) // rows_per_mod, 0, 0)),
            pl.BlockSpec((1, d), lambda i, *_: (0, 0)),
            pl.BlockSpec((1, d), lambda i, *_: (0, 0)),
        ],
        out_specs=pl.BlockSpec((tb, d), lambda i, *_: (i, 0)),
        scratch_shapes=[pltpu.VMEM((LOCAL_ROWS, d), BF16), pltpu.SemaphoreType.DMA(())],
    )
    return pl.pallas_call(
        functools.partial(_combine_kernel, block0=block0),
        out_shape=jax.ShapeDtypeStruct((t, d), F32),
        grid_spec=grid_spec,
        compiler_params=pltpu.CompilerParams(
            dimension_semantics=("arbitrary",), vmem_limit_bytes=VMEM_LIMIT),
        name="moe_combine_ln",
    )(tabs["goff"], tabs["nu"], tabs["lst"], tabs["tot"], ys, ridx_all, tabs["lrow"], x1, g2,
      ln_g, ln_b)


def _num_expert_tiles(t):
    nb = t // BLOCK_TOKENS
    return -(-(TOP_K * t + nb * N_EXPERTS * (STRIP_ROWS - 1)) // EXPERT_TILE) + N_EXPERTS


def _route_tables(ridx_all):
    t = ridx_all.shape[0]
    tb, tme = BLOCK_TOKENS, EXPERT_TILE
    nb = t // tb
    experts = jnp.arange(N_EXPERTS, dtype=I32)
    onehot = jnp.sum((ridx_all[:, :TOP_K, None] == experts[None, None, :]).astype(I32), axis=1)
    u = STRIP_ROWS
    nu = (jnp.sum(onehot.reshape(nb, tb, N_EXPERTS), axis=1) + u - 1) // u
    lstu = jnp.cumsum(nu, axis=1) - nu
    tot = jnp.sum(nu, axis=1)
    gu = jnp.sum(nu, axis=0)
    tu = tme // u
    tiles = (gu + tu - 1) // tu
    tile_end = jnp.cumsum(tiles)
    tile_first = tile_end - tiles
    goffu = (tile_first * tu)[None, :] + jnp.cumsum(nu, axis=0) - nu
    ntiles = _num_expert_tiles(t)
    jt = jnp.arange(ntiles, dtype=I32)
    te = jnp.minimum(jnp.sum((jt[:, None] >= tile_end[None, :]).astype(I32), axis=1), N_EXPERTS - 1)
    nv = jnp.clip(gu[te] * u - (jt - tile_first[te]) * tme, 0, tme)
    nv = jnp.where(jt < tile_end[-1], nv, 0)
    first = ((jt == tile_first[te]) & (nv > 0)).astype(I32)
    lst = (lstu * u).astype(I32)
    return dict(
        goff=(goffu * u).astype(I32).reshape(-1), nu=nu.astype(I32).reshape(-1), lst=lst.reshape(-1),
        tot=tot.astype(I32), te=te.astype(I32), nv=nv.astype(I32), first=first,
        lcol=lst.astype(F32)[:, :, None],
        lrow=_pad_lanes(lst.astype(F32))[:, None, :],
    )


def _rot_cols(w):
    q = QK_ROPE // 4
    w1, w2, w3, w4 = w[..., 0:q], w[..., q:2 * q], w[..., 2 * q:3 * q], w[..., 3 * q:4 * q]
    return jnp.concatenate([-w2, w1, -w4, w3], axis=-1)


def _pad_lanes(w, width=LANES):
    return jnp.pad(w, [(0, 0)] * (w.ndim - 1) + [(0, width - w.shape[-1])])


def _pack_layer(l, w_in, b_in, q_norm, kv_norm, w_uq, w_uk, w_uv, w_oa, conv_w, conv_b, w_ob, w_o,
                ln1_g, ln1_b, ln2_g, ln2_b, w_router, b_router):
    wi, bi = w_in[l], b_in[l][None, :]

    def pack(m):
        kr = m[:, OFF_KR:OFF_CX]
        return jnp.concatenate([
            m[:, OFF_QA:OFF_KR], _pad_lanes(kr), _pad_lanes(_rot_cols(kr)),
            m[:, OFF_CX:OFF_GA], m[:, OFF_GA:IN_WIDTH]], axis=1)

    uq = w_uq[l].reshape(Q_LORA, N_HEADS, QK_NOPE + QK_ROPE)
    uq_nope = uq[:, :, :QK_NOPE].reshape(Q_LORA, N_HEADS * QK_NOPE)
    uq_rope = uq[:, :, QK_NOPE:]
    wq = jnp.concatenate([
        uq_nope,
        _pad_lanes(uq_rope).reshape(Q_LORA, N_HEADS * LANES),
        _pad_lanes(_rot_cols(uq_rope)).reshape(Q_LORA, N_HEADS * LANES)], axis=1)

    eye = jnp.eye(N_HEADS, dtype=F32)
    uk = w_uk[l].reshape(KV_LORA, N_HEADS, QK_NOPE)
    bduk = jnp.einsum("chd,hg->hdgc", uk, eye).reshape(N_HEADS * QK_NOPE, N_HEADS * KV_LORA)
    uv = w_uv[l].reshape(KV_LORA, N_HEADS, V_DIM)
    bduv = jnp.einsum("chd,hg->hcgd", uv, eye).reshape(N_HEADS * KV_LORA, N_HEADS * V_DIM)

    return dict(
        w_pack=pack(wi).astype(BF16), b_pack=pack(bi),
        q_norm=q_norm[l][None, :], kv_norm=kv_norm[l][None, :],
        wq=wq.astype(BF16), bduk=bduk.astype(BF16), bduv=bduv.astype(BF16),
        w_oa=w_oa[l].astype(BF16), w_ob=w_ob[l].astype(BF16), w_o=w_o[l].astype(BF16),
        conv_w=jnp.pad(conv_w[l], ((0, 8 - CONV_K), (0, 0))), conv_b=conv_b[l][None, :],
        ln1_g=ln1_g[l][None, :], ln1_b=ln1_b[l][None, :],
        ln2_g=ln2_g[l][None, :], ln2_b=ln2_b[l][None, :],
        w_router=_pad_lanes(w_router[l]),
        b_router=jnp.pad(b_router[l], (0, LANES - N_EXPERTS), constant_values=NEG_BIG)[None, :],
    )


def _rope_tables(n):
    n_rows = n // GRID_W
    row = jnp.broadcast_to(jnp.arange(n_rows, dtype=F32)[:, None], (n_rows, GRID_W)).reshape(-1)
    col = jnp.broadcast_to(jnp.arange(GRID_W, dtype=F32)[None, :], (n_rows, GRID_W)).reshape(-1)
    half = QK_ROPE // 2
    freqs = ROPE_THETA ** (-jnp.arange(0, half, 2, dtype=F32) / half)
    ang_r = row[:, None] * freqs
    ang_c = col[:, None] * freqs
    ang = jnp.concatenate([ang_r, ang_r, ang_c, ang_c], axis=-1)
    return _pad_lanes(jnp.cos(ang)), _pad_lanes(jnp.sin(ang))


def kernel(x, c, ctx, c_ctx, w_ada, b_ada, w_in, b_in, q_norm, kv_norm, w_uq, w_uk, w_uv, w_oa,
           conv_w, conv_b, w_ob, w_o, ln1_g, ln1_b, ln2_g, ln2_b, w_router, b_router,
           w_gu, b_gu, w_down, b_down):
    b, n, d = x.shape
    nc = ctx.shape[1]
    assert ROW_TILE == BLOCK_TOKENS and BLOCK_TOKENS % nc == 0 and (b * nc) % BLOCK_TOKENS == 0
    cvec = jnp.concatenate([c, c_ctx[None, :], jnp.zeros((8 - b - 1, d), F32)], axis=0)
    mod = _modulation(cvec, w_ada, b_ada)

    cos_x, sin_x = _rope_tables(n)
    cos_c = _pad_lanes(jnp.ones((nc, QK_ROPE), F32))
    sin_c = jnp.zeros((nc, LANES), F32)

    for l in range(DEPTH):
        update_ctx = l < DEPTH - 1
        lw = _pack_layer(l, w_in, b_in, q_norm, kv_norm, w_uq, w_uk, w_uv, w_oa, conv_w, conv_b,
                         w_ob, w_o, ln1_g, ln1_b, ln2_g, ln2_b, w_router, b_router)
        sh1, sc1, g1, sh2, sc2, g2 = [m[:, None, :] for m in jnp.split(mod[l], 6, axis=-1)]
        xm = lambda m: m[:b]
        cm = lambda m: jnp.broadcast_to(m[b:b + 1], (b, 1, d))

        q_x, kt_x, v_x, zc_x, bg_x, gs_x = _in_projection(
            x, 1.0 + xm(sc1), xm(sh1), cos_x, sin_x, lw, ROW_TILE)
        q_c, kt_c, v_c, zc_c, bg_c, gs_c = _in_projection(
            ctx, 1.0 + cm(sc1), cm(sh1), cos_c, sin_c, lw, nc)
        kt_all = jnp.concatenate([kt_c, kt_x], axis=2)
        v_all = jnp.concatenate([v_c, v_x], axis=1)
        o_x = _latent_attention(q_x, kt_all, v_all, Q_TILE, KV_CHUNK)
        x1, hx, ridx, ridxt = _merge_ln_route(
            o_x, zc_x, bg_x, gs_x, x, xm(g1), 1.0 + xm(sc2), xm(sh2), lw, ROW_TILE)
        hx_all = hx.reshape(b * n, XROW_W)
        ridx_all = ridx.reshape(b * n, LANES)
        if update_ctx:
            o_c = _latent_attention(q_c, kt_c, v_c, nc, nc)
            c1, hxc, ridx_c, ridxt_c = _merge_ln_route(
                o_c, zc_c, bg_c, gs_c, ctx, cm(g1), 1.0 + cm(sc2), cm(sh2), lw, nc)
            per_block = BLOCK_TOKENS // nc
            ridxt_c = ridxt_c.reshape(b // per_block, per_block, 8, nc).transpose(0, 2, 1, 3)
            hx_all = jnp.concatenate([hx_all, hxc.reshape(b * nc, XROW_W)], axis=0)
            ridx_all = jnp.concatenate([ridx_all, ridx_c.reshape(b * nc, LANES)], axis=0)
            ridxt = jnp.concatenate([ridxt, ridxt_c.reshape(b // per_block, 8, BLOCK_TOKENS)], axis=0)

        tabs = _route_tables(ridx_all)
        ntiles = _num_expert_tiles(ridx_all.shape[0])
        xs = _dispatch(tabs, ridxt, hx_all, ntiles * EXPERT_TILE)
        ys = _expert_ffn(tabs, xs, l, w_gu, b_gu, w_down, b_down, ntiles)
        x = _combine_ln(tabs, ys, ridx_all, x1.reshape(b * n, d), xm(g2), n,
                        lw["ln2_g"], lw["ln2_b"], 0).reshape(b, n, d)
        if update_ctx:
            ctx = _combine_ln(tabs, ys, ridx_all, c1.reshape(b * nc, d), cm(g2), nc,
                              lw["ln2_g"], lw["ln2_b"], (b * n) // BLOCK_TOKENS).reshape(b, nc, d)
    return x
```
